```python
import math
import jax, jax.numpy as jnp
from jax import lax
import numpy as np

D_MODEL = 1024
BATCH = 32
SEQ = 2048
DEPTH = 2

MIX_WIDTH = D_MODEL
DN_WIDTH = MIX_WIDTH // 2
DN_HEADS = 4
DN_HEAD_DIM = DN_WIDTH // DN_HEADS
CONV_WIDTH = 4
CHUNK = 64
SSM_WIDTH = MIX_WIDTH - DN_WIDTH
SSM_GROUP = 16
SSM_GROUPS = SSM_WIDTH // SSM_GROUP
SSM_STATE = 64
EPS = 1e-6
COL_QKV = 3 * DN_WIDTH
COL_ZDN = COL_QKV + DN_WIDTH
COL_BETA = COL_ZDN + DN_HEADS
COL_ALPHA = COL_BETA + DN_HEADS
COL_U = COL_ALPHA + SSM_WIDTH
IN_COLS = COL_U + SSM_WIDTH

kernel_name = "hybrid_gdn_s5_parallel_heads"


def rms_norm(x, g):
    xf = x.astype(jnp.float32)
    y = xf * lax.rsqrt(jnp.mean(xf * xf, axis=-1, keepdims=True) + EPS)
    return (y * g.astype(jnp.float32)).astype(x.dtype)


def l2_normalize(x):
    return x * lax.rsqrt(jnp.sum(x * x, axis=-1, keepdims=True) + EPS)


def causal_dwconv(x, w):
    c = x.shape[-1]
    return lax.conv_general_dilated(
        x, w[:, None, :].astype(x.dtype), window_strides=(1,),
        padding=[(CONV_WIDTH - 1, 0)],
        dimension_numbers=("NWC", "WIO", "NWC"), feature_group_count=c)


def gated_delta_rule(q, k, v, g, beta):
    bsz, s, h, dk = q.shape
    dv = v.shape[-1]
    n = s // CHUNK
    def chunks(t):
        t = t.reshape((bsz, n, CHUNK, h) + t.shape[3:])
        return jnp.moveaxis(t, 3, 1)
    q = chunks(q * (dk ** -0.5))
    k = chunks(k)
    v = chunks(v)
    gc = jnp.cumsum(chunks(g), axis=-1)
    beta = chunks(beta)
    causal = jnp.tril(jnp.ones((CHUNK, CHUNK), dtype=bool))
    strict = jnp.tril(jnp.ones((CHUNK, CHUNK), dtype=bool), -1)
    diff = gc[..., :, None] - gc[..., None, :]
    decay = jnp.where(causal, jnp.exp(jnp.where(causal, diff, 0.0)), 0.0)
    kb = k * beta[..., None]
    vb = v * beta[..., None]
    a_mat = jnp.where(strict, jnp.einsum("bhncd,bhnsd->bhncs", kb, k) * decay, 0.0)
    lower = a_mat + jnp.eye(CHUNK, dtype=a_mat.dtype)
    rhs = jnp.concatenate([vb, kb * jnp.exp(gc)[..., None]], axis=-1)
    sol = lax.linalg.triangular_solve(lower, rhs, left_side=True, lower=True, unit_diagonal=True)
    u, w = sol[..., :dv], sol[..., dv:]
    intra = jnp.where(causal, jnp.einsum("bhncd,bhnsd->bhncs", q, k) * decay, 0.0)

    def step(state, inp):
        q_i, k_i, u_i, w_i, g_i, a_i = inp
        v_new = u_i - jnp.einsum("bhcd,bhdv->bhcv", w_i, state)
        o = (jnp.einsum("bhcd,bhdv->bhcv", q_i * jnp.exp(g_i)[..., None], state)
             + jnp.einsum("bhcs,bhsv->bhcv", a_i, v_new))
        g_last = g_i[..., -1]
        k_dec = k_i * jnp.exp(g_last[..., None] - g_i)[..., None]
        state = state * jnp.exp(g_last)[..., None, None] + jnp.einsum("bhcd,bhcv->bhdv", k_dec, v_new)
        return state, o

    xs = tuple(jnp.moveaxis(t, 2, 0) for t in (q, k, u, w, gc, intra))
    state0 = jnp.zeros((bsz, h, dk, dv), jnp.float32)
    _, o = lax.scan(step, state0, xs)
    o = jnp.transpose(o, (1, 0, 3, 2, 4))
    return o.reshape(bsz, s, h, dv)


def s5_scan(u, a_re, a_im, b_re, b_im, c_re, c_im, d, log_dt):
    bsz, s, _ = u.shape
    ug = u.reshape(bsz, s, SSM_GROUPS, SSM_GROUP)
    dt = jnp.exp(log_dt.astype(jnp.float32))[:, None]
    a_re = a_re.astype(jnp.float32)
    a_im = a_im.astype(jnp.float32)
    mag = jnp.exp(a_re * dt)
    ang = a_im * dt
    lb_re, lb_im = mag * jnp.cos(ang), mag * jnp.sin(ang)
    den = a_re * a_re + a_im * a_im
    num_re, num_im = lb_re - 1.0, lb_im
    coef_re = (num_re * a_re + num_im * a_im) / den
    coef_im = (num_im * a_re - num_re * a_im) / den
    b_re = b_re.astype(jnp.float32)
    b_im = b_im.astype(jnp.float32)
    bb_re = coef_re[..., None] * b_re - coef_im[..., None] * b_im
    bb_im = coef_re[..., None] * b_im + coef_im[..., None] * b_re
    bu_re = jnp.einsum("bsgh,gph->sbgp", ug, bb_re)
    bu_im = jnp.einsum("bsgh,gph->sbgp", ug, bb_im)
    lam_re = jnp.broadcast_to(lb_re[None, None], (s, 1, SSM_GROUPS, SSM_STATE))
    lam_im = jnp.broadcast_to(lb_im[None, None], (s, 1, SSM_GROUPS, SSM_STATE))

    def combine(e1, e2):
        a1r, a1i, b1r, b1i = e1
        a2r, a2i, b2r, b2i = e2
        return (a2r * a1r - a2i * a1i, a2r * a1i + a2i * a1r,
                a2r * b1r - a2i * b1i + b2r, a2r * b1i + a2i * b1r + b2i)

    _, _, st_re, st_im = lax.associative_scan(combine, (lam_re, lam_im, bu_re, bu_im), axis=0)
    y = (jnp.einsum("sbgp,ghp->bsgh", st_re, c_re.astype(jnp.float32))
         - jnp.einsum("sbgp,ghp->bsgh", st_im, c_im.astype(jnp.float32)))
    return y.reshape(bsz, s, SSM_WIDTH) + d.astype(jnp.float32) * u


def hybrid_layer(x, pre_g, post_g, w_in, conv_w, dn_a_log, dn_dt_bias, dn_norm_g,
                 ssm_a_re, ssm_a_im, ssm_b_re, ssm_b_im, ssm_c_re, ssm_c_im, ssm_d,
                 ssm_log_dt, glu_w, glu_b, w_out):
    bsz, s, _ = x.shape
    h = rms_norm(x, pre_g)
    p = h @ w_in
    qkv = p[..., :COL_QKV]
    z_dn = p[..., COL_QKV:COL_ZDN]
    beta_logit = p[..., COL_ZDN:COL_BETA]
    alpha = p[..., COL_BETA:COL_ALPHA]
    u_ssm = p[..., COL_ALPHA:COL_U]
    z_ssm = p[..., COL_U:]

    qkv = jax.nn.silu(causal_dwconv(qkv, conv_w)).astype(jnp.float32)
    q = l2_normalize(qkv[..., :DN_WIDTH].reshape(bsz, s, DN_HEADS, DN_HEAD_DIM))
    k = l2_normalize(qkv[..., DN_WIDTH:2 * DN_WIDTH].reshape(bsz, s, DN_HEADS, DN_HEAD_DIM))
    v = qkv[..., 2 * DN_WIDTH:].reshape(bsz, s, DN_HEADS, DN_HEAD_DIM)
    beta = jax.nn.sigmoid(beta_logit.astype(jnp.float32))
    g = -jnp.exp(dn_a_log.astype(jnp.float32)) * jax.nn.softplus(
        alpha.astype(jnp.float32) + dn_dt_bias.astype(jnp.float32))
    o = gated_delta_rule(q, k, v, g, beta)
    o = rms_norm(o, dn_norm_g) * jax.nn.silu(
        z_dn.astype(jnp.float32).reshape(bsz, s, DN_HEADS, DN_HEAD_DIM))
    o_dn = o.reshape(bsz, s, DN_WIDTH).astype(x.dtype)

    y = s5_scan(u_ssm.astype(jnp.float32), ssm_a_re, ssm_a_im, ssm_b_re, ssm_b_im,
                ssm_c_re, ssm_c_im, ssm_d, ssm_log_dt)
    y = jax.nn.gelu(y)
    gl = y @ glu_w.astype(jnp.float32) + glu_b.astype(jnp.float32)
    y = gl[..., :SSM_WIDTH] * jax.nn.sigmoid(gl[..., SSM_WIDTH:])
    o_ssm = (y * jax.nn.silu(z_ssm.astype(jnp.float32))).astype(x.dtype)

    mix = jnp.concatenate([o_dn, o_ssm], axis=-1) @ w_out
    return x + rms_norm(mix, post_g)


def setup_inputs(seed: int = 0) -> dict:
    key = jax.random.key(seed)
    ks = jax.random.split(key, 20)
    f32 = jnp.float32
    nrm = lambda k, shp, sc: sc * jax.random.normal(k, shp, f32)
    log_lo, log_hi = math.log(1e-3), math.log(1e-1)
    dn_dt = jnp.exp(jax.random.uniform(ks[6], (DEPTH, DN_HEADS), f32, log_lo, log_hi))
    return {
        "x": jax.random.normal(ks[0], (BATCH, SEQ, D_MODEL), f32),
        "pre_norm_g": 1.0 + nrm(ks[1], (DEPTH, D_MODEL), 0.05),
        "post_norm_g": 1.0 + nrm(ks[2], (DEPTH, D_MODEL), 0.05),
        "w_in": nrm(ks[3], (DEPTH, D_MODEL, IN_COLS), D_MODEL ** -0.5),
        "conv_w": nrm(ks[4], (DEPTH, CONV_WIDTH, COL_QKV), CONV_WIDTH ** -0.5),
        "dn_a_log": jnp.log(jax.random.uniform(ks[5], (DEPTH, DN_HEADS), f32, 1.0, 16.0)),
        "dn_dt_bias": dn_dt + jnp.log(-jnp.expm1(-dn_dt)),
        "dn_norm_g": 1.0 + nrm(ks[7], (DEPTH, DN_HEAD_DIM), 0.05),
        "ssm_a_re": -0.5 + nrm(ks[8], (DEPTH, SSM_GROUPS, SSM_STATE), 0.01),
        "ssm_a_im": math.pi * jnp.arange(SSM_STATE, dtype=f32) + nrm(ks[9], (DEPTH, SSM_GROUPS, SSM_STATE), 0.01),
        "ssm_b_re": nrm(ks[10], (DEPTH, SSM_GROUPS, SSM_STATE, SSM_GROUP), (2 * SSM_GROUP) ** -0.5),
        "ssm_b_im": nrm(ks[11], (DEPTH, SSM_GROUPS, SSM_STATE, SSM_GROUP), (2 * SSM_GROUP) ** -0.5),
        "ssm_c_re": nrm(ks[12], (DEPTH, SSM_GROUPS, SSM_GROUP, SSM_STATE), SSM_STATE ** -0.5),
        "ssm_c_im": nrm(ks[13], (DEPTH, SSM_GROUPS, SSM_GROUP, SSM_STATE), SSM_STATE ** -0.5),
        "ssm_d": nrm(ks[14], (DEPTH, SSM_WIDTH), 1.0),
        "ssm_log_dt": jax.random.uniform(ks[15], (DEPTH, SSM_GROUPS), f32, log_lo, log_hi),
        "glu_w": nrm(ks[16], (DEPTH, SSM_WIDTH, 2 * SSM_WIDTH), SSM_WIDTH ** -0.5),
        "glu_b": nrm(ks[17], (DEPTH, 2 * SSM_WIDTH), 0.01),
        "w_out": nrm(ks[18], (DEPTH, MIX_WIDTH, D_MODEL), MIX_WIDTH ** -0.5),
    }


def reference(x, pre_norm_g, post_norm_g, w_in, conv_w, dn_a_log, dn_dt_bias, dn_norm_g,
              ssm_a_re, ssm_a_im, ssm_b_re, ssm_b_im, ssm_c_re, ssm_c_im, ssm_d,
              ssm_log_dt, glu_w, glu_b, w_out):
    for l in range(DEPTH):
        x = hybrid_layer(x, pre_norm_g[l], post_norm_g[l], w_in[l], conv_w[l], dn_a_log[l],
                         dn_dt_bias[l], dn_norm_g[l], ssm_a_re[l], ssm_a_im[l], ssm_b_re[l],
                         ssm_b_im[l], ssm_c_re[l], ssm_c_im[l], ssm_d[l], ssm_log_dt[l],
                         glu_w[l], glu_b[l], w_out[l])
    return x
```

```python
import functools
import math

import jax
import jax.numpy as jnp
from jax import lax
from jax.experimental import pallas as pl
from jax.experimental.pallas import tpu as pltpu

F32 = jnp.float32
BF16 = jnp.bfloat16

DN_HEADS = 4
DN_HEAD_DIM = 128
DN_WIDTH = DN_HEADS * DN_HEAD_DIM
CONV_WIDTH = 4
CHUNK = 64
HROWS = DN_HEADS * CHUNK
SSM_GROUP = 16
SSM_GROUPS = 32
SSM_STATE = 64
SSM_WIDTH = SSM_GROUP * SSM_GROUPS
SSM_L = 16
SSM_PAIRS = SSM_GROUPS // 2
SSM_ROW = SSM_L * SSM_GROUP
EPS = 1e-6
LANES = 128
CONV_HDR = 8
VMEM_LIMIT_BYTES = 56 * 1024 * 1024


def _dot(a, b):
    return jnp.dot(a, b, preferred_element_type=F32)


def _dot_nt(a, b):
    return lax.dot_general(a, b, (((1,), (1,)), ((), ())), preferred_element_type=F32)


def _dot_tn(a, b):
    return lax.dot_general(a, b, (((0,), (0,)), ((), ())), preferred_element_type=F32)


def _split3(a):
    hi = a.astype(BF16)
    r1 = a - hi.astype(F32)
    mid = r1.astype(BF16)
    lo = (r1 - mid.astype(F32)).astype(BF16)
    return hi, mid, lo


def _dot01_right(a, w01):
    hi, mid, lo = _split3(a)
    return _dot(hi, w01) + _dot(mid, w01) + _dot(lo, w01)


def _dot01_left(w01, a):
    hi, mid, lo = _split3(a)
    return _dot(w01, hi) + _dot(w01, mid) + _dot(w01, lo)


def _sigmoid(x):
    return 1.0 / (1.0 + jnp.exp(-x))


def _silu(x):
    return x * _sigmoid(x)


def _softplus(x):
    return jnp.maximum(x, 0.0) + jnp.log(1.0 + jnp.exp(-jnp.abs(x)))


def _gelu_tanh(x):
    c = math.sqrt(2.0 / math.pi)
    return 0.5 * x * (1.0 + jnp.tanh(c * (x + 0.044715 * (x * x * x))))


def _piece_transpose8(vs):
    lane = lax.broadcasted_iota(jnp.int32, vs[0].shape, 1)
    piece = lane // SSM_GROUP
    for d in (4, 2, 1):
        keep = (piece & d) == 0
        new = list(vs)
        for a in range(8):
            if a & d == 0:
                b = a + d
                new[a] = jnp.where(keep, vs[a], pltpu.roll(vs[b], d * SSM_GROUP, 1))
                new[b] = jnp.where(keep, pltpu.roll(vs[a], LANES - d * SSM_GROUP, 1), vs[b])
        vs = new
    return vs


def _layer_kernel(x_ref, pre_g_ref, post_g_ref, wqkv_ref, wz_ref, wu_ref, wzs_ref, wsm_ref,
                  conv_ref, dnvec_ref, normg_ref, tri_ref, esel_ref, maskc_ref, masks_ref, eye_ref,
                  mssm_ref, bpow_ref, cpow_ref, lam_ref, gluw_ref, glub_ref, wout_ref,
                  o_ref,
                  qkv_scr, q_scr, k_scr, v_scr, gb_scr, bb_scr, odn_scr, state_scr,
                  u_scr, y_scr, bre_scr, bim_scr, sre_scr, sim_scr, carry_scr, *, ts):
    t = pl.program_id(1)
    nchunk = ts // CHUNK
    nblk = ts // SSM_L

    @pl.when(t == 0)
    def _():
        qkv_scr[0:CONV_HDR, :] = jnp.zeros((CONV_HDR, 3 * DN_WIDTH), F32)
        state_scr[...] = jnp.zeros(state_scr.shape, F32)
        carry_scr[...] = jnp.zeros(carry_scr.shape, F32)

    x = x_ref[0]
    ms = jnp.mean(x * x, axis=-1, keepdims=True)
    h = (x * lax.rsqrt(ms + EPS) * pre_g_ref[...]).astype(BF16)

    qkv_scr[CONV_HDR:CONV_HDR + ts, :] = _dot(h, wqkv_ref[...])
    u_val = _dot(h, wu_ref[...])
    for vcol in range(SSM_WIDTH // LANES):
        u_scr[vcol] = u_val[:, vcol * LANES:(vcol + 1) * LANES]
    small = _dot(h, wsm_ref[...])

    acc = conv_ref[CONV_WIDTH - 1:CONV_WIDTH, :] * qkv_scr[CONV_HDR:CONV_HDR + ts, :]
    for j in range(CONV_WIDTH - 1):
        off = CONV_HDR - (CONV_WIDTH - 1) + j
        acc = acc + conv_ref[j:j + 1, :] * qkv_scr[off:off + ts, :]
    qkv_scr[0:CONV_HDR, :] = qkv_scr[ts:ts + CONV_HDR, :]
    c = _silu(acc)
    for hd in range(DN_HEADS):
        lo = hd * DN_HEAD_DIM
        qh = c[:, lo:lo + DN_HEAD_DIM]
        kh = c[:, DN_WIDTH + lo:DN_WIDTH + lo + DN_HEAD_DIM]
        qn = lax.rsqrt(jnp.sum(qh * qh, axis=-1, keepdims=True) + EPS) * (DN_HEAD_DIM ** -0.5)
        kn = lax.rsqrt(jnp.sum(kh * kh, axis=-1, keepdims=True) + EPS)
        q_scr[:, lo:lo + DN_HEAD_DIM] = qh * qn
        k_scr[:, lo:lo + DN_HEAD_DIM] = kh * kn
    v_scr[...] = c[:, 2 * DN_WIDTH:]

    lane = lax.broadcasted_iota(jnp.int32, small.shape, 1)
    is_g = (lane >= DN_HEADS) & (lane < 2 * DN_HEADS)
    neg_a = dnvec_ref[0:1, :]
    dt_b = dnvec_ref[1:2, :]
    gates = jnp.where(is_g, neg_a * _softplus(small + dt_b), _sigmoid(small))
    gcum = _dot01_left(tri_ref[...], gates)
    gsel = jnp.where(is_g, gcum, gates)
    bcast = _dot01_right(gsel, esel_ref[...])
    bb_scr[...] = bcast[:, :DN_WIDTH]
    gb_scr[...] = bcast[:, DN_WIDTH:]

    def chunk_body(ci, carry):
        r0 = pl.multiple_of(ci * CHUNK, CHUNK)
        rows = pl.ds(r0, CHUNK)

        def stack(ref):
            return jnp.concatenate(
                [ref[rows, hd * DN_HEAD_DIM:(hd + 1) * DN_HEAD_DIM] for hd in range(DN_HEADS)], axis=0)

        kst = stack(k_scr)
        qst = stack(q_scr)
        vst = stack(v_scr)
        gcol = stack(gb_scr)
        bcol = stack(bb_scr)
        kb = kst * bcol
        vb = vst * bcol
        egc = jnp.exp(gcol)
        grow = jnp.transpose(gcol)[0:1, :]
        gcol2 = jnp.concatenate([gcol, gcol], axis=1)
        maskc = maskc_ref[...]
        diff = jnp.where(maskc > 0.0, gcol2 - grow, 0.0)
        decay = jnp.exp(diff) * maskc
        kst_b = kst.astype(BF16)
        kk = _dot_nt(jnp.concatenate([kb, qst], axis=0).astype(BF16), kst_b)
        a_mat = kk[:HROWS] * decay * masks_ref[...]
        intra = kk[HROWS:] * decay
        tinv = eye_ref[...] - a_mat
        pw = a_mat.astype(BF16)
        pw = _dot(pw, pw)
        for lvl in range(5):
            pwb = pw.astype(BF16)
            if lvl < 4:
                both = _dot(pwb, jnp.concatenate([tinv.astype(BF16), pwb], axis=1))
                tinv = tinv + both[:, :HROWS]
                pw = both[:, HROWS:]
            else:
                tinv = tinv + _dot(pwb, tinv.astype(BF16))
        sol = _dot(tinv.astype(BF16), jnp.concatenate([vb, kb * egc], axis=1).astype(BF16))
        u_mat = sol[:, :DN_HEAD_DIM]
        w_mat = sol[:, DN_HEAD_DIM:].astype(BF16)
        qg = (qst * egc).astype(BF16)

        vnews = []
        oinit = []
        for hd in range(DN_HEADS):
            hs = slice(hd * CHUNK, (hd + 1) * CHUNK)
            s_b = state_scr[hd].astype(BF16)
            vnews.append(u_mat[hs] - _dot(w_mat[hs], s_b))
            oinit.append(_dot(qg[hs], s_b))
        vnew = jnp.concatenate(vnews, axis=0)
        vnew_b = vnew.astype(BF16)
        o_st = jnp.concatenate(oinit, axis=0) + _dot(intra.astype(BF16), vnew_b)
        for hd in range(DN_HEADS):
            hs = slice(hd * CHUNK, (hd + 1) * CHUNK)
            g_h = gcol[hs]
            g_last = g_h[CHUNK - 1:CHUNK, :]
            kdec = (kst[hs] * jnp.exp(g_last - g_h)).astype(BF16)
            state_scr[hd] = state_scr[hd] * jnp.exp(g_last) + _dot_tn(kdec, vnew_b[hs])
            odn_scr[rows, hd * DN_HEAD_DIM:(hd + 1) * DN_HEAD_DIM] = o_st[hs]
        return carry

    lax.fori_loop(0, nchunk, chunk_body, 0)

    z_dn = _dot(h, wz_ref[...])
    o_parts = []
    for hd in range(DN_HEADS):
        sl = slice(hd * DN_HEAD_DIM, (hd + 1) * DN_HEAD_DIM)
        oh = odn_scr[:, sl]
        on = oh * lax.rsqrt(jnp.mean(oh * oh, axis=-1, keepdims=True) + EPS) * normg_ref[...]
        o_parts.append((on * _silu(z_dn[:, sl])).astype(BF16))

    xg = [[None, None] for _ in range(SSM_GROUPS)]
    for vcol in range(SSM_WIDTH // LANES):
        for jh in range(2):
            outs = _piece_transpose8(
                [u_scr[vcol, pl.ds(jh * 8 + j8, nblk, stride=SSM_L), :] for j8 in range(8)])
            for g8 in range(8):
                xg[vcol * 8 + g8][jh] = outs[g8]
    xgb = [jnp.concatenate(xg[g], axis=1).astype(BF16) for g in range(SSM_GROUPS)]
    for pr in range(SSM_PAIRS):
        xp = jnp.concatenate([xgb[2 * pr], xgb[2 * pr + 1]], axis=1)
        bpair = _dot(xp, bpow_ref[pr])
        bre_scr[:, pr * LANES:(pr + 1) * LANES] = bpair[:, :LANES]
        bim_scr[:, pr * LANES:(pr + 1) * LANES] = bpair[:, LANES:]

    lre = lam_ref[0:1, :]
    lim = lam_ref[1:2, :]

    def blk_body(cb, carry):
        sre, sim = carry
        sre_scr[pl.ds(cb, 1), :] = sre
        sim_scr[pl.ds(cb, 1), :] = sim
        bre = bre_scr[pl.ds(cb, 1), :]
        bim = bim_scr[pl.ds(cb, 1), :]
        return (lre * sre - lim * sim + bre, lre * sim + lim * sre + bim)

    sre, sim = lax.fori_loop(0, nblk, blk_body, (carry_scr[0:1, :], carry_scr[1:2, :]))
    carry_scr[0:1, :] = sre
    carry_scr[1:2, :] = sim

    yg = [None] * SSM_GROUPS
    for pr in range(SSM_PAIRS):
        sp = jnp.concatenate([sre_scr[:, pr * LANES:(pr + 1) * LANES],
                              sim_scr[:, pr * LANES:(pr + 1) * LANES]], axis=1).astype(BF16)
        ys = _dot(sp, cpow_ref[pr])
        for k in range(2):
            g = 2 * pr + k
            yg[g] = ys[:, k * SSM_ROW:(k + 1) * SSM_ROW] + _dot(xgb[g], mssm_ref[g])
    for vcol in range(SSM_WIDTH // LANES):
        for ih in range(2):
            outs = _piece_transpose8(
                [yg[vcol * 8 + g8][:, ih * LANES:(ih + 1) * LANES] for g8 in range(8)])
            for i8 in range(8):
                y_scr[vcol, pl.ds(ih * 8 + i8, nblk, stride=SSM_L), :] = outs[i8]

    y_val = jnp.concatenate([y_scr[vcol] for vcol in range(SSM_WIDTH // LANES)], axis=1)
    ge = _gelu_tanh(y_val).astype(BF16)
    gl = _dot(ge, gluw_ref[...]) + glub_ref[...]
    z_ssm = _dot(h, wzs_ref[...])
    o_ssm = (gl[:, :SSM_WIDTH] * _sigmoid(gl[:, SSM_WIDTH:]) * _silu(z_ssm)).astype(BF16)

    mix_in = jnp.concatenate(o_parts + [o_ssm], axis=1)
    mix = _dot(mix_in, wout_ref[...])
    mn = mix * lax.rsqrt(jnp.mean(mix * mix, axis=-1, keepdims=True) + EPS) * post_g_ref[...]
    o_ref[0] = x + mn


def _ssm_matrices(a_re, a_im, b_re, b_im, c_re, c_im, d, log_dt):
    hp = lax.Precision.HIGHEST
    dt = jnp.exp(log_dt.astype(F32))[:, None]
    a_re = a_re.astype(F32)
    a_im = a_im.astype(F32)
    mag = jnp.exp(a_re * dt)
    ang = a_im * dt
    lb_re, lb_im = mag * jnp.cos(ang), mag * jnp.sin(ang)
    den = a_re * a_re + a_im * a_im
    num_re, num_im = lb_re - 1.0, lb_im
    coef_re = (num_re * a_re + num_im * a_im) / den
    coef_im = (num_im * a_re - num_re * a_im) / den
    b_re = b_re.astype(F32)
    b_im = b_im.astype(F32)
    bb_re = coef_re[..., None] * b_re - coef_im[..., None] * b_im
    bb_im = coef_re[..., None] * b_im + coef_im[..., None] * b_re
    taus = jnp.arange(SSM_L + 1, dtype=F32)[:, None, None]
    pmag = jnp.exp(a_re * dt * taus)
    pang = a_im * dt * taus
    lp_re, lp_im = pmag * jnp.cos(pang), pmag * jnp.sin(pang)
    c_re = c_re.astype(F32)
    c_im = c_im.astype(F32)
    cl_re = c_re[None] * lp_re[:, :, None, :] - c_im[None] * lp_im[:, :, None, :]
    cl_im = c_re[None] * lp_im[:, :, None, :] + c_im[None] * lp_re[:, :, None, :]
    kern = (jnp.einsum("tghp,gpk->tghk", cl_re[:SSM_L], bb_re, precision=hp)
            - jnp.einsum("tghp,gpk->tghk", cl_im[:SSM_L], bb_im, precision=hp))
    kern = kern.at[0].add(d.astype(F32).reshape(SSM_GROUPS, SSM_GROUP)[:, :, None]
                          * jnp.eye(SSM_GROUP, dtype=F32)[None])
    ii = jnp.arange(SSM_L)
    lag = ii[None, :] - ii[:, None]
    kt = kern[jnp.clip(lag, 0, SSM_L - 1)]
    kt = jnp.where((lag >= 0)[:, :, None, None, None], kt, 0.0)
    mssm = jnp.transpose(kt, (2, 0, 4, 1, 3)).reshape(SSM_GROUPS, SSM_ROW, SSM_ROW)
    lr = lp_re[:SSM_L][::-1]
    li = lp_im[:SSM_L][::-1]
    bp_re = lr[:, :, :, None] * bb_re[None] - li[:, :, :, None] * bb_im[None]
    bp_im = lr[:, :, :, None] * bb_im[None] + li[:, :, :, None] * bb_re[None]
    bp_re = jnp.transpose(bp_re, (1, 0, 3, 2)).reshape(SSM_GROUPS, SSM_ROW, SSM_STATE)
    bp_im = jnp.transpose(bp_im, (1, 0, 3, 2)).reshape(SSM_GROUPS, SSM_ROW, SSM_STATE)
    zb = jnp.zeros_like(bp_re)
    ev, od = slice(0, None, 2), slice(1, None, 2)
    bpow = jnp.concatenate([
        jnp.concatenate([bp_re[ev], zb[ev], bp_im[ev], zb[ev]], axis=2),
        jnp.concatenate([zb[od], bp_re[od], zb[od], bp_im[od]], axis=2)], axis=1)
    co_re = jnp.transpose(cl_re[1:], (1, 3, 0, 2)).reshape(SSM_GROUPS, SSM_STATE, SSM_ROW)
    co_im = -jnp.transpose(cl_im[1:], (1, 3, 0, 2)).reshape(SSM_GROUPS, SSM_STATE, SSM_ROW)
    zc = jnp.zeros_like(co_re)
    cpow = jnp.concatenate([
        jnp.concatenate([co_re[ev], zc[ev]], axis=2),
        jnp.concatenate([zc[od], co_re[od]], axis=2),
        jnp.concatenate([co_im[ev], zc[ev]], axis=2),
        jnp.concatenate([zc[od], co_im[od]], axis=2)], axis=1)
    lam = jnp.stack([lp_re[SSM_L].reshape(-1), lp_im[SSM_L].reshape(-1)], axis=0)
    return mssm.astype(BF16), bpow.astype(BF16), cpow.astype(BF16), lam


def _constants(ts):
    r = jnp.arange(ts)
    tri = ((r[:, None] // CHUNK == r[None, :] // CHUNK) & (r[:, None] >= r[None, :])).astype(BF16)
    lane = jnp.arange(LANES)
    col = jnp.arange(2 * DN_WIDTH)
    esel = (lane[:, None] == col[None, :] // DN_HEAD_DIM).astype(BF16)
    hr = jnp.arange(HROWS)
    same = hr[:, None] // CHUNK == hr[None, :] // CHUNK
    maskc = (same & (hr[:, None] >= hr[None, :])).astype(F32)
    masks = (same & (hr[:, None] > hr[None, :])).astype(F32)
    eye = jnp.eye(HROWS, dtype=F32)
    return tri, esel, maskc, masks, eye


def _layer(x, ts, pre_g, post_g, w_in, conv_w, a_log, dt_bias, norm_g, ssm, glu_w, glu_b, w_out):
    bsz, s, d = x.shape
    c_qkv = 3 * DN_WIDTH
    c_z = c_qkv + DN_WIDTH
    c_beta = c_z + DN_HEADS
    c_alpha = c_beta + DN_HEADS
    c_u = c_alpha + SSM_WIDTH
    wqkv = w_in[:, :c_qkv].astype(BF16)
    wz = w_in[:, c_qkv:c_z].astype(BF16)
    wu = w_in[:, c_alpha:c_u].astype(BF16)
    wzs = w_in[:, c_u:].astype(BF16)
    wsm = jnp.pad(w_in[:, c_z:c_alpha], ((0, 0), (0, LANES - 2 * DN_HEADS))).astype(BF16)
    dnvec = jnp.zeros((2, LANES), F32)
    dnvec = dnvec.at[0, DN_HEADS:2 * DN_HEADS].set(-jnp.exp(a_log.astype(F32)))
    dnvec = dnvec.at[1, DN_HEADS:2 * DN_HEADS].set(dt_bias.astype(F32))
    mssm, bpow, cpow, lam = ssm
    tri, esel, maskc, masks, eye = _constants(ts)
    nblk = ts // SSM_L
    nstate = SSM_GROUPS * SSM_STATE

    def full(a):
        nd = a.ndim
        return pl.BlockSpec(a.shape, lambda b, t, _nd=nd: (0,) * _nd)

    operands = [
        pre_g.reshape(1, d).astype(F32), post_g.reshape(1, d).astype(F32),
        wqkv, wz, wu, wzs, wsm, conv_w.astype(F32), dnvec, norm_g.reshape(1, DN_HEAD_DIM).astype(F32),
        tri, esel, maskc, masks, eye, mssm, bpow, cpow, lam,
        glu_w.astype(BF16), glu_b.reshape(1, -1).astype(F32), w_out.astype(BF16)]
    scratch = [
        pltpu.VMEM((ts + CONV_HDR, c_qkv), F32),
        pltpu.VMEM((ts, DN_WIDTH), F32),
        pltpu.VMEM((ts, DN_WIDTH), F32),
        pltpu.VMEM((ts, DN_WIDTH), F32),
        pltpu.VMEM((ts, DN_WIDTH), F32),
        pltpu.VMEM((ts, DN_WIDTH), F32),
        pltpu.VMEM((ts, DN_WIDTH), F32),
        pltpu.VMEM((DN_HEADS, DN_HEAD_DIM, DN_HEAD_DIM), F32),
        pltpu.VMEM((SSM_WIDTH // LANES, ts, LANES), F32),
        pltpu.VMEM((SSM_WIDTH // LANES, ts, LANES), F32),
        pltpu.VMEM((nblk, nstate), F32),
        pltpu.VMEM((nblk, nstate), F32),
        pltpu.VMEM((nblk, nstate), F32),
        pltpu.VMEM((nblk, nstate), F32),
        pltpu.VMEM((2, nstate), F32),
    ]
    return pl.pallas_call(
        functools.partial(_layer_kernel, ts=ts),
        out_shape=jax.ShapeDtypeStruct(x.shape, x.dtype),
        grid=(bsz, s // ts),
        in_specs=[pl.BlockSpec((1, ts, d), lambda b, t: (b, t, 0))] + [full(a) for a in operands],
        out_specs=pl.BlockSpec((1, ts, d), lambda b, t: (b, t, 0)),
        scratch_shapes=scratch,
        compiler_params=pltpu.CompilerParams(
            dimension_semantics=("arbitrary", "arbitrary"),
            vmem_limit_bytes=VMEM_LIMIT_BYTES),
        name="hybrid_layer",
    )(x, *operands)


def _pick_tile(s):
    for ts in (512, 256, 128, 64):
        if s % ts == 0:
            return ts
    raise ValueError(f"sequence length {s} must be a multiple of {CHUNK}")


def kernel(x, pre_norm_g, post_norm_g, w_in, conv_w, dn_a_log, dn_dt_bias, dn_norm_g, ssm_a_re, ssm_a_im, ssm_b_re, ssm_b_im, ssm_c_re, ssm_c_im, ssm_d, ssm_log_dt, glu_w, glu_b, w_out):
    depth = w_in.shape[0]
    ts = _pick_tile(x.shape[1])
    for l in range(depth):
        ssm = _ssm_matrices(ssm_a_re[l], ssm_a_im[l], ssm_b_re[l], ssm_b_im[l], ssm_c_re[l],
                            ssm_c_im[l], ssm_d[l], ssm_log_dt[l])
        x = _layer(x, ts, pre_norm_g[l], post_norm_g[l], w_in[l], conv_w[l], dn_a_log[l],
                   dn_dt_bias[l], dn_norm_g[l], ssm, glu_w[l], glu_b[l], w_out[l])
    return x
```

```python
import functools
import math

import jax
import jax.numpy as jnp
from jax import lax
from jax.experimental import pallas as pl
from jax.experimental.pallas import tpu as pltpu

F32 = jnp.float32
BF16 = jnp.bfloat16

DN_HEADS = 4
DN_HEAD_DIM = 128
DN_WIDTH = DN_HEADS * DN_HEAD_DIM
CONV_WIDTH = 4
CHUNK = 64
HROWS = DN_HEADS * CHUNK
SSM_GROUP = 16
SSM_GROUPS = 32
SSM_STATE = 64
SSM_WIDTH = SSM_GROUP * SSM_GROUPS
SSM_L = 16
SSM_PAIRS = SSM_GROUPS // 2
SSM_ROW = SSM_L * SSM_GROUP
EPS = 1e-6
LANES = 128
NORM_ROWS = 32
GDN_GROUP = 4
CONV_HDR = 8
VMEM_LIMIT_BYTES = 56 * 1024 * 1024


def _dot(a, b):
    return jnp.dot(a, b, preferred_element_type=F32)


def _dot_nt(a, b):
    return lax.dot_general(a, b, (((1,), (1,)), ((), ())), preferred_element_type=F32)


def _dot_tn(a, b):
    return lax.dot_general(a, b, (((0,), (0,)), ((), ())), preferred_element_type=F32)


def _split2(a):
    hi = a.astype(BF16)
    mid = (a - hi.astype(F32)).astype(BF16)
    return hi, mid


def _sigmoid(x):
    return 1.0 / (1.0 + jnp.exp(-x))


def _silu(x):
    return x * _sigmoid(x)


def _softplus(x):
    return jnp.maximum(x, 0.0) + jnp.log(1.0 + jnp.exp(-jnp.abs(x)))


def _gelu_tanh(x):
    c = math.sqrt(2.0 / math.pi)
    return 0.5 * x * (1.0 + jnp.tanh(c * (x + 0.044715 * (x * x * x))))


def _piece_transpose8(vs):
    lane = lax.broadcasted_iota(jnp.int32, vs[0].shape, 1)
    piece = lane // SSM_GROUP
    for d in (4, 2, 1):
        keep = (piece & d) == 0
        new = list(vs)
        for a in range(8):
            if a & d == 0:
                b = a + d
                new[a] = jnp.where(keep, vs[a], pltpu.roll(vs[b], d * SSM_GROUP, 1))
                new[b] = jnp.where(keep, pltpu.roll(vs[a], LANES - d * SSM_GROUP, 1), vs[b])
        vs = new
    return vs


def _layer_kernel(x_ref, pre_g_ref, post_g_ref, wqkv_ref, wz_ref, wu_ref, wzs_ref, wsm_ref,
                  conv_ref, dnvec_ref, normg_ref, tri_ref, eselg_ref, eselb_ref, causalcat_ref, strictcat_ref,
                  samef_ref, sameb_ref, eyecat_ref,
                  mssm_ref, bpow_ref, cpow_ref, lam_ref, gluw_ref, glub_ref, wout_ref,
                  o_ref,
                  h_scr, qkv_scr, q_scr, k_scr, v_scr, gb_scr, bb_scr, odn_scr, state_scr,
                  u1_scr, wq_scr, intra_scr, kdec_scr, elast_scr,
                  u_scr, y_scr, xg_scr, yg_scr, bre_scr, bim_scr, sre_scr, sim_scr, carry_scr, *, ts):
    t = pl.program_id(1)
    nchunk = ts // CHUNK
    nblk = ts // SSM_L

    @pl.when(t == 0)
    def _():
        qkv_scr[0:CONV_HDR, :] = jnp.zeros((CONV_HDR, 3 * DN_WIDTH), F32)
        state_scr[...] = jnp.zeros(state_scr.shape, F32)
        carry_scr[...] = jnp.zeros(carry_scr.shape, F32)

    for r in range(0, ts, NORM_ROWS):
        xb = x_ref[0, r:r + NORM_ROWS, :]
        ms = jnp.mean(xb * xb, axis=-1, keepdims=True)
        h_scr[r:r + NORM_ROWS, :] = (xb * lax.rsqrt(ms + EPS) * pre_g_ref[...]).astype(BF16)
    h = h_scr[...]

    qkv_scr[CONV_HDR:CONV_HDR + ts, :] = _dot(h, wqkv_ref[...])
    u_val = _dot(h, wu_ref[...])
    for vcol in range(SSM_WIDTH // LANES):
        u_scr[vcol] = u_val[:, vcol * LANES:(vcol + 1) * LANES]
    small = _dot(h, wsm_ref[...])

    for cb in range(3 * DN_WIDTH // LANES):
        cols = slice(cb * LANES, (cb + 1) * LANES)
        for r in range(0, ts, CHUNK):
            base = CONV_HDR + r
            acc = conv_ref[CONV_WIDTH - 1:CONV_WIDTH, cols] * qkv_scr[base:base + CHUNK, cols]
            for j in range(CONV_WIDTH - 1):
                off = base - (CONV_WIDTH - 1) + j
                acc = acc + conv_ref[j:j + 1, cols] * qkv_scr[off:off + CHUNK, cols]
            c = _silu(acc)
            if cb < DN_HEADS:
                c = c * (lax.rsqrt(jnp.sum(c * c, axis=-1, keepdims=True) + EPS) * (DN_HEAD_DIM ** -0.5))
                q_scr[r:r + CHUNK, cols] = c
            elif cb < 2 * DN_HEADS:
                c = c * lax.rsqrt(jnp.sum(c * c, axis=-1, keepdims=True) + EPS)
                k_scr[r:r + CHUNK, cb * LANES - DN_WIDTH:(cb + 1) * LANES - DN_WIDTH] = c
            else:
                v_scr[r:r + CHUNK, cb * LANES - 2 * DN_WIDTH:(cb + 1) * LANES - 2 * DN_WIDTH] = c
    qkv_scr[0:CONV_HDR, :] = qkv_scr[ts:ts + CONV_HDR, :]

    lane = lax.broadcasted_iota(jnp.int32, small.shape, 1)
    is_g = (lane >= DN_HEADS) & (lane < 2 * DN_HEADS)
    neg_a = dnvec_ref[0:1, :]
    dt_b = dnvec_ref[1:2, :]
    gates = jnp.where(is_g, neg_a * _softplus(small + dt_b), _sigmoid(small))
    g_hi, g_mid = _split2(gates)
    gcum = _dot(tri_ref[...], g_hi) + _dot(tri_ref[...], g_mid)
    c_hi, c_mid = _split2(gcum)
    gb_scr[...] = _dot(c_hi, eselg_ref[...]) + _dot(c_mid, eselg_ref[...])
    bb_scr[...] = _dot(g_hi, eselb_ref[...])

    sameb = sameb_ref[...]
    lane_c = lax.broadcasted_iota(jnp.int32, (CHUNK, LANES), 1)
    low_half = lane_c < CHUNK

    def blockdiag(xc):
        return jnp.concatenate([xc.astype(BF16)] * DN_HEADS, axis=0) * sameb

    def headcat(m):
        parts = [m[hd * CHUNK:(hd + 1) * CHUNK] * samef_ref[hd * CHUNK:(hd + 1) * CHUNK, :]
                 for hd in range(DN_HEADS)]
        return (parts[0] + parts[1]) + (parts[2] + parts[3])

    def chunk_setup(ci):
        rows = slice(ci * CHUNK, (ci + 1) * CHUNK)

        def stack(ref):
            return jnp.concatenate(
                [ref[rows, hd * DN_HEAD_DIM:(hd + 1) * DN_HEAD_DIM] for hd in range(DN_HEADS)], axis=0)

        kst = stack(k_scr)
        qst = stack(q_scr)
        vst = stack(v_scr)
        gcol = stack(gb_scr)
        bcol = stack(bb_scr)
        kb = kst * bcol
        vb = vst * bcol
        egc = jnp.exp(gcol)
        grow = jnp.transpose(gcol)[0:1, :]
        gcol_cat = jnp.concatenate(
            [jnp.where(low_half, gb_scr[rows, 0:LANES], gb_scr[rows, LANES:2 * LANES]),
             jnp.where(low_half, gb_scr[rows, 2 * LANES:3 * LANES], gb_scr[rows, 3 * LANES:4 * LANES])],
            axis=1)
        causal = causalcat_ref[...]
        decay = jnp.exp(jnp.where(causal > 0.0, gcol_cat - grow, 0.0)) * causal
        kk = _dot_nt(jnp.concatenate([kb, qst], axis=0).astype(BF16), kst.astype(BF16))
        a_cat = headcat(kk[:HROWS]) * decay * strictcat_ref[...]
        intra_scr[ci] = (headcat(kk[HROWS:]) * decay).astype(BF16)
        rhs = jnp.concatenate([vb, kb * egc], axis=1).astype(BF16)
        qg = (qst * egc).astype(BF16)
        kdecs = []
        for hd in range(DN_HEADS):
            hs = slice(hd * CHUNK, (hd + 1) * CHUNK)
            g_h = gcol[hs]
            g_last = g_h[CHUNK - 1:CHUNK, :]
            kdecs.append(kst[hs] * jnp.exp(g_last - g_h))
            elast_scr[ci, hd:hd + 1, :] = jnp.exp(g_last)
        kdec_scr[ci] = jnp.concatenate(kdecs, axis=0).astype(BF16)
        return a_cat, rhs, qg

    for g0 in range(0, nchunk, GDN_GROUP):
        group = range(g0, min(g0 + GDN_GROUP, nchunk))
        setup = [chunk_setup(ci) for ci in group]
        t_cats = [eyecat_ref[...] - a_cat for a_cat, _, _ in setup]
        p_cats = [_dot(a_cat.astype(BF16), blockdiag(a_cat)) for a_cat, _, _ in setup]
        for lvl in range(5):
            for k in range(len(setup)):
                p_bd = blockdiag(p_cats[k])
                if lvl < 4:
                    both = _dot(jnp.concatenate([t_cats[k], p_cats[k]], axis=0).astype(BF16), p_bd)
                    t_cats[k] = t_cats[k] + both[:CHUNK]
                    p_cats[k] = both[CHUNK:]
                else:
                    t_cats[k] = t_cats[k] + _dot(t_cats[k].astype(BF16), p_bd)
        for k, ci in enumerate(group):
            _, rhs, qg = setup[k]
            sol = _dot(blockdiag(t_cats[k]), rhs)
            u1_scr[ci] = sol[:, :DN_HEAD_DIM]
            w_mat = sol[:, DN_HEAD_DIM:].astype(BF16)
            for hd in range(DN_HEADS):
                hs = slice(hd * CHUNK, (hd + 1) * CHUNK)
                wq_scr[ci * DN_HEADS + hd] = jnp.concatenate([w_mat[hs], qg[hs]], axis=0)

    xg = [[None, None] for _ in range(SSM_GROUPS)]
    for vcol in range(SSM_WIDTH // LANES):
        for jh in range(2):
            outs = _piece_transpose8(
                [u_scr[vcol, pl.ds(jh * 8 + j8, nblk, stride=SSM_L), :] for j8 in range(8)])
            for g8 in range(8):
                xg[vcol * 8 + g8][jh] = outs[g8]
    for g in range(SSM_GROUPS):
        xg_scr[g] = jnp.concatenate(xg[g], axis=1).astype(BF16)

    def ssm_state_in(pr):
        xp = jnp.concatenate([xg_scr[2 * pr], xg_scr[2 * pr + 1]], axis=1)
        bpair = _dot(xp, bpow_ref[pr])
        bre_scr[:, pr * LANES:(pr + 1) * LANES] = bpair[:, :LANES]
        bim_scr[:, pr * LANES:(pr + 1) * LANES] = bpair[:, LANES:]

    def ssm_state_out(pr):
        sp = jnp.concatenate([sre_scr[:, pr * LANES:(pr + 1) * LANES],
                              sim_scr[:, pr * LANES:(pr + 1) * LANES]], axis=1).astype(BF16)
        ys = _dot(sp, cpow_ref[pr])
        for k in range(2):
            g = 2 * pr + k
            yg_scr[g] = ys[:, k * SSM_ROW:(k + 1) * SSM_ROW] + _dot(xg_scr[g], mssm_ref[g])

    def delta_chunk(ci):
        rows = slice(ci * CHUNK, (ci + 1) * CHUNK)
        u_mat = u1_scr[ci]
        vnews = []
        oinit = []
        for hd in range(DN_HEADS):
            hs = slice(hd * CHUNK, (hd + 1) * CHUNK)
            ws = _dot(wq_scr[ci * DN_HEADS + hd], state_scr[hd].astype(BF16))
            vnews.append(u_mat[hs] - ws[:CHUNK])
            oinit.append(ws[CHUNK:])
        vnew_b = jnp.concatenate(vnews, axis=0).astype(BF16)
        o_st = jnp.concatenate(oinit, axis=0) + _dot(blockdiag(intra_scr[ci]), vnew_b)
        kdec = kdec_scr[ci]
        for hd in range(DN_HEADS):
            hs = slice(hd * CHUNK, (hd + 1) * CHUNK)
            state_scr[hd] = (state_scr[hd] * elast_scr[ci, hd:hd + 1, :]
                             + _dot_tn(kdec[hs], vnew_b[hs]))
            odn_scr[rows, hd * DN_HEAD_DIM:(hd + 1) * DN_HEAD_DIM] = o_st[hs]

    half = nchunk // 2
    in_per = -(-SSM_PAIRS // max(half, 1))
    out_per = -(-SSM_PAIRS // max(nchunk - half, 1))
    for ci in range(half):
        delta_chunk(ci)
        for pr in range(ci * in_per, min((ci + 1) * in_per, SSM_PAIRS)):
            ssm_state_in(pr)
    for pr in range(half * in_per, SSM_PAIRS):
        ssm_state_in(pr)

    lre = lam_ref[0:1, :]
    lim = lam_ref[1:2, :]
    sre = carry_scr[0:1, :]
    sim = carry_scr[1:2, :]
    for cb in range(nblk):
        sre_scr[cb:cb + 1, :] = sre
        sim_scr[cb:cb + 1, :] = sim
        bre = bre_scr[cb:cb + 1, :]
        bim = bim_scr[cb:cb + 1, :]
        sre, sim = lre * sre - lim * sim + bre, lre * sim + lim * sre + bim
    carry_scr[0:1, :] = sre
    carry_scr[1:2, :] = sim

    for ci in range(half, nchunk):
        delta_chunk(ci)
        for pr in range((ci - half) * out_per, min((ci - half + 1) * out_per, SSM_PAIRS)):
            ssm_state_out(pr)
    for pr in range((nchunk - half) * out_per, SSM_PAIRS):
        ssm_state_out(pr)

    z_dn = _dot(h, wz_ref[...])
    o_parts = []
    for hd in range(DN_HEADS):
        sl = slice(hd * DN_HEAD_DIM, (hd + 1) * DN_HEAD_DIM)
        oh = odn_scr[:, sl]
        on = oh * lax.rsqrt(jnp.mean(oh * oh, axis=-1, keepdims=True) + EPS) * normg_ref[...]
        o_parts.append((on * _silu(z_dn[:, sl])).astype(BF16))

    yg = [yg_scr[g] for g in range(SSM_GROUPS)]
    for vcol in range(SSM_WIDTH // LANES):
        for ih in range(2):
            outs = _piece_transpose8(
                [yg[vcol * 8 + g8][:, ih * LANES:(ih + 1) * LANES] for g8 in range(8)])
            for i8 in range(8):
                y_scr[vcol, pl.ds(ih * 8 + i8, nblk, stride=SSM_L), :] = outs[i8]

    y_val = jnp.concatenate([y_scr[vcol] for vcol in range(SSM_WIDTH // LANES)], axis=1)
    ge = _gelu_tanh(y_val).astype(BF16)
    gl = _dot(ge, gluw_ref[...]) + glub_ref[...]
    z_ssm = _dot(h, wzs_ref[...])
    o_ssm = (gl[:, :SSM_WIDTH] * _sigmoid(gl[:, SSM_WIDTH:]) * _silu(z_ssm)).astype(BF16)

    mix_in = jnp.concatenate(o_parts + [o_ssm], axis=1)
    mix = _dot(mix_in, wout_ref[...])
    mn = mix * lax.rsqrt(jnp.mean(mix * mix, axis=-1, keepdims=True) + EPS) * post_g_ref[...]
    o_ref[0] = x_ref[0] + mn


def _ssm_matrices(a_re, a_im, b_re, b_im, c_re, c_im, d, log_dt):
    hp = lax.Precision.HIGHEST
    dt = jnp.exp(log_dt.astype(F32))[:, None]
    a_re = a_re.astype(F32)
    a_im = a_im.astype(F32)
    mag = jnp.exp(a_re * dt)
    ang = a_im * dt
    lb_re, lb_im = mag * jnp.cos(ang), mag * jnp.sin(ang)
    den = a_re * a_re + a_im * a_im
    num_re, num_im = lb_re - 1.0, lb_im
    coef_re = (num_re * a_re + num_im * a_im) / den
    coef_im = (num_im * a_re - num_re * a_im) / den
    b_re = b_re.astype(F32)
    b_im = b_im.astype(F32)
    bb_re = coef_re[..., None] * b_re - coef_im[..., None] * b_im
    bb_im = coef_re[..., None] * b_im + coef_im[..., None] * b_re
    taus = jnp.arange(SSM_L + 1, dtype=F32)[:, None, None]
    pmag = jnp.exp(a_re * dt * taus)
    pang = a_im * dt * taus
    lp_re, lp_im = pmag * jnp.cos(pang), pmag * jnp.sin(pang)
    c_re = c_re.astype(F32)
    c_im = c_im.astype(F32)
    cl_re = c_re[None] * lp_re[:, :, None, :] - c_im[None] * lp_im[:, :, None, :]
    cl_im = c_re[None] * lp_im[:, :, None, :] + c_im[None] * lp_re[:, :, None, :]
    kern = (jnp.einsum("tghp,gpk->tghk", cl_re[:SSM_L], bb_re, precision=hp)
            - jnp.einsum("tghp,gpk->tghk", cl_im[:SSM_L], bb_im, precision=hp))
    kern = kern.at[0].add(d.astype(F32).reshape(SSM_GROUPS, SSM_GROUP)[:, :, None]
                          * jnp.eye(SSM_GROUP, dtype=F32)[None])
    ii = jnp.arange(SSM_L)
    lag = ii[None, :] - ii[:, None]
    kt = kern[jnp.clip(lag, 0, SSM_L - 1)]
    kt = jnp.where((lag >= 0)[:, :, None, None, None], kt, 0.0)
    mssm = jnp.transpose(kt, (2, 0, 4, 1, 3)).reshape(SSM_GROUPS, SSM_ROW, SSM_ROW)
    lr = lp_re[:SSM_L][::-1]
    li = lp_im[:SSM_L][::-1]
    bp_re = lr[:, :, :, None] * bb_re[None] - li[:, :, :, None] * bb_im[None]
    bp_im = lr[:, :, :, None] * bb_im[None] + li[:, :, :, None] * bb_re[None]
    bp_re = jnp.transpose(bp_re, (1, 0, 3, 2)).reshape(SSM_GROUPS, SSM_ROW, SSM_STATE)
    bp_im = jnp.transpose(bp_im, (1, 0, 3, 2)).reshape(SSM_GROUPS, SSM_ROW, SSM_STATE)
    zb = jnp.zeros_like(bp_re)
    ev, od = slice(0, None, 2), slice(1, None, 2)
    bpow = jnp.concatenate([
        jnp.concatenate([bp_re[ev], zb[ev], bp_im[ev], zb[ev]], axis=2),
        jnp.concatenate([zb[od], bp_re[od], zb[od], bp_im[od]], axis=2)], axis=1)
    co_re = jnp.transpose(cl_re[1:], (1, 3, 0, 2)).reshape(SSM_GROUPS, SSM_STATE, SSM_ROW)
    co_im = -jnp.transpose(cl_im[1:], (1, 3, 0, 2)).reshape(SSM_GROUPS, SSM_STATE, SSM_ROW)
    zc = jnp.zeros_like(co_re)
    cpow = jnp.concatenate([
        jnp.concatenate([co_re[ev], zc[ev]], axis=2),
        jnp.concatenate([zc[od], co_re[od]], axis=2),
        jnp.concatenate([co_im[ev], zc[ev]], axis=2),
        jnp.concatenate([zc[od], co_im[od]], axis=2)], axis=1)
    lam = jnp.stack([lp_re[SSM_L].reshape(-1), lp_im[SSM_L].reshape(-1)], axis=0)
    return mssm.astype(BF16), bpow.astype(BF16), cpow.astype(BF16), lam


def _constants(ts):
    r = jnp.arange(ts)
    tri = ((r[:, None] // CHUNK == r[None, :] // CHUNK) & (r[:, None] >= r[None, :])).astype(BF16)
    lane = jnp.arange(LANES)
    col = jnp.arange(DN_WIDTH)
    eselb = (lane[:, None] == col[None, :] // DN_HEAD_DIM).astype(BF16)
    eselg = (lane[:, None] == DN_HEADS + col[None, :] // DN_HEAD_DIM).astype(BF16)
    hr = jnp.arange(HROWS)
    same = hr[:, None] // CHUNK == hr[None, :] // CHUNK
    ri = jnp.arange(CHUNK)[:, None]
    cj = hr[None, :] % CHUNK
    causalcat = (ri >= cj).astype(F32)
    strictcat = (ri > cj).astype(F32)
    eyecat = (ri == cj).astype(F32)
    return tri, eselg, eselb, causalcat, strictcat, same.astype(F32), same.astype(BF16), eyecat


def _layer(x, ts, pre_g, post_g, w_in, conv_w, a_log, dt_bias, norm_g, ssm, glu_w, glu_b, w_out):
    bsz, s, d = x.shape
    c_qkv = 3 * DN_WIDTH
    c_z = c_qkv + DN_WIDTH
    c_beta = c_z + DN_HEADS
    c_alpha = c_beta + DN_HEADS
    c_u = c_alpha + SSM_WIDTH
    wqkv = w_in[:, :c_qkv].astype(BF16)
    wz = w_in[:, c_qkv:c_z].astype(BF16)
    wu = w_in[:, c_alpha:c_u].astype(BF16)
    wzs = w_in[:, c_u:].astype(BF16)
    wsm = jnp.pad(w_in[:, c_z:c_alpha], ((0, 0), (0, LANES - 2 * DN_HEADS))).astype(BF16)
    dnvec = jnp.zeros((2, LANES), F32)
    dnvec = dnvec.at[0, DN_HEADS:2 * DN_HEADS].set(-jnp.exp(a_log.astype(F32)))
    dnvec = dnvec.at[1, DN_HEADS:2 * DN_HEADS].set(dt_bias.astype(F32))
    mssm, bpow, cpow, lam = ssm
    consts = _constants(ts)
    nblk = ts // SSM_L
    nstate = SSM_GROUPS * SSM_STATE

    def full(a):
        nd = a.ndim
        return pl.BlockSpec(a.shape, lambda b, t, _nd=nd: (0,) * _nd)

    operands = [
        pre_g.reshape(1, d).astype(F32), post_g.reshape(1, d).astype(F32),
        wqkv, wz, wu, wzs, wsm, conv_w.astype(F32), dnvec, norm_g.reshape(1, DN_HEAD_DIM).astype(F32),
        *consts, mssm, bpow, cpow, lam,
        glu_w.astype(BF16), glu_b.reshape(1, -1).astype(F32), w_out.astype(BF16)]
    scratch = [
        pltpu.VMEM((ts, d), BF16),
        pltpu.VMEM((ts + CONV_HDR, c_qkv), F32),
        pltpu.VMEM((ts, DN_WIDTH), F32),
        pltpu.VMEM((ts, DN_WIDTH), F32),
        pltpu.VMEM((ts, DN_WIDTH), F32),
        pltpu.VMEM((ts, DN_WIDTH), F32),
        pltpu.VMEM((ts, DN_WIDTH), F32),
        pltpu.VMEM((ts, DN_WIDTH), F32),
        pltpu.VMEM((DN_HEADS, DN_HEAD_DIM, DN_HEAD_DIM), F32),
        pltpu.VMEM((ts // CHUNK, HROWS, DN_HEAD_DIM), F32),
        pltpu.VMEM((ts // CHUNK * DN_HEADS, 2 * CHUNK, DN_HEAD_DIM), BF16),
        pltpu.VMEM((ts // CHUNK, CHUNK, HROWS), BF16),
        pltpu.VMEM((ts // CHUNK, HROWS, DN_HEAD_DIM), BF16),
        pltpu.VMEM((ts // CHUNK, 8, DN_HEAD_DIM), F32),
        pltpu.VMEM((SSM_WIDTH // LANES, ts, LANES), F32),
        pltpu.VMEM((SSM_WIDTH // LANES, ts, LANES), F32),
        pltpu.VMEM((SSM_GROUPS, nblk, SSM_ROW), BF16),
        pltpu.VMEM((SSM_GROUPS, nblk, SSM_ROW), F32),
        pltpu.VMEM((nblk, nstate), F32),
        pltpu.VMEM((nblk, nstate), F32),
        pltpu.VMEM((nblk, nstate), F32),
        pltpu.VMEM((nblk, nstate), F32),
        pltpu.VMEM((2, nstate), F32),
    ]
    return pl.pallas_call(
        functools.partial(_layer_kernel, ts=ts),
        out_shape=jax.ShapeDtypeStruct(x.shape, x.dtype),
        grid=(bsz, s // ts),
        in_specs=[pl.BlockSpec((1, ts, d), lambda b, t: (b, t, 0))] + [full(a) for a in operands],
        out_specs=pl.BlockSpec((1, ts, d), lambda b, t: (b, t, 0)),
        scratch_shapes=scratch,
        compiler_params=pltpu.CompilerParams(
            dimension_semantics=("arbitrary", "arbitrary"),
            vmem_limit_bytes=VMEM_LIMIT_BYTES),
        name="hybrid_layer",
    )(x, *operands)


def _pick_tile(s):
    for ts in (512, 256, 128, 64):
        if s % ts == 0:
            return ts
    raise ValueError(f"sequence length {s} must be a multiple of {CHUNK}")


def kernel(x, pre_norm_g, post_norm_g, w_in, conv_w, dn_a_log, dn_dt_bias, dn_norm_g, ssm_a_re, ssm_a_im, ssm_b_re, ssm_b_im, ssm_c_re, ssm_c_im, ssm_d, ssm_log_dt, glu_w, glu_b, w_out):
    depth = w_in.shape[0]
    ts = _pick_tile(x.shape[1])
    for l in range(depth):
        ssm = _ssm_matrices(ssm_a_re[l], ssm_a_im[l], ssm_b_re[l], ssm_b_im[l], ssm_c_re[l],
                            ssm_c_im[l], ssm_d[l], ssm_log_dt[l])
        x = _layer(x, ts, pre_norm_g[l], post_norm_g[l], w_in[l], conv_w[l], dn_a_log[l],
                   dn_dt_bias[l], dn_norm_g[l], ssm, glu_w[l], glu_b[l], w_out[l])
    return x
```

```python
import functools
import math

import jax
import jax.numpy as jnp
from jax import lax
from jax.experimental import pallas as pl
from jax.experimental.pallas import tpu as pltpu

F32 = jnp.float32
BF16 = jnp.bfloat16

DN_HEADS = 4
DN_HEAD_DIM = 128
DN_WIDTH = DN_HEADS * DN_HEAD_DIM
CONV_WIDTH = 4
CHUNK = 64
HROWS = DN_HEADS * CHUNK
SSM_GROUP = 16
SSM_GROUPS = 32
SSM_STATE = 64
SSM_WIDTH = SSM_GROUP * SSM_GROUPS
SSM_L = 16
SSM_PAIRS = SSM_GROUPS // 2
SSM_ROW = SSM_L * SSM_GROUP
EPS = 1e-6
NEG_LOG2E = -1.4426950408889634
LANES = 128
NORM_ROWS = 32
GDN_GROUP = 8
CONV_HDR = 8
VMEM_LIMIT_BYTES = 56 * 1024 * 1024


def _dot(a, b):
    return jnp.dot(a, b, preferred_element_type=F32)


def _dot_nt(a, b):
    return lax.dot_general(a, b, (((1,), (1,)), ((), ())), preferred_element_type=F32)


def _dot_tn(a, b):
    return lax.dot_general(a, b, (((0,), (0,)), ((), ())), preferred_element_type=F32)


def _split2(a):
    hi = a.astype(BF16)
    mid = (a - hi.astype(F32)).astype(BF16)
    return hi, mid


def _sigmoid(x):
    return 1.0 / (1.0 + jnp.exp2(x * NEG_LOG2E))


def _silu(x):
    return x * _sigmoid(x)


def _softplus(x):
    return jnp.maximum(x, 0.0) + jnp.log(1.0 + jnp.exp(-jnp.abs(x)))


def _gelu_tanh(x):
    c = math.sqrt(2.0 / math.pi)
    return 0.5 * x * (1.0 + jnp.tanh(c * (x + 0.044715 * (x * x * x))))


def _piece_transpose8(vs):
    lane = lax.broadcasted_iota(jnp.int32, vs[0].shape, 1)
    piece = lane // SSM_GROUP
    for d in (4, 2, 1):
        keep = (piece & d) == 0
        new = list(vs)
        for a in range(8):
            if a & d == 0:
                b = a + d
                new[a] = jnp.where(keep, vs[a], pltpu.roll(vs[b], d * SSM_GROUP, 1))
                new[b] = jnp.where(keep, pltpu.roll(vs[a], LANES - d * SSM_GROUP, 1), vs[b])
        vs = new
    return vs


_DONE = object()


def _weave(main, side):
    side_done = False
    for _ in main:
        if not side_done:
            side_done = next(side, _DONE) is _DONE
    for _ in side:
        pass


def _layer_kernel(x_ref, pre_g_ref, post_g_ref, wqkv_ref, wz_ref, wu_ref, wzs_ref, wsm_ref,
                  conv_ref, dnvec_ref, normg_ref, tri_ref, eselg_ref, eselb_ref, causalcat_ref, strictcat_ref,
                  eyecat_ref, mssm_ref, bpow_ref, cpow_ref, lam_ref, gluw_ref, glub_ref, wout_ref,
                  o_ref,
                  h_scr, qkv_scr, q_scr, k_scr, v_scr, gb_scr, bb_scr, odn_scr, state_scr,
                  u1_scr, wq_scr, intra_scr, kdec_scr, elast_scr,
                  u_scr, y_scr, xg_scr, yg_scr, zdn_scr, zs_scr, ossm_scr,
                  bre_scr, bim_scr, sre_scr, sim_scr, carry_scr, *, ts):
    t = pl.program_id(1)
    nchunk = ts // CHUNK
    nblk = ts // SSM_L
    row_halves = [slice(0, ts)]

    @pl.when(t == 0)
    def _():
        qkv_scr[0:CONV_HDR, :] = jnp.zeros((CONV_HDR, 3 * DN_WIDTH), F32)
        state_scr[...] = jnp.zeros(state_scr.shape, F32)
        carry_scr[...] = jnp.zeros(carry_scr.shape, F32)

    for r in range(0, ts, NORM_ROWS):
        xb = x_ref[0, r:r + NORM_ROWS, :]
        ms = jnp.mean(xb * xb, axis=-1, keepdims=True)
        h_scr[r:r + NORM_ROWS, :] = (xb * lax.rsqrt(ms + EPS) * pre_g_ref[...]).astype(BF16)
    h = h_scr[...]

    small = _dot(h, wsm_ref[...])
    qkv_scr[CONV_HDR:CONV_HDR + ts, :] = _dot(h, wqkv_ref[...])
    u_val = _dot(h, wu_ref[...])
    for vcol in range(SSM_WIDTH // LANES):
        u_scr[vcol] = u_val[:, vcol * LANES:(vcol + 1) * LANES]
    zdn_scr[...] = _dot(h, wz_ref[...])

    for cb in range(3 * DN_WIDTH // LANES):
        cols = slice(cb * LANES, (cb + 1) * LANES)
        for r in range(0, ts, CHUNK):
            base = CONV_HDR + r
            acc = conv_ref[CONV_WIDTH - 1:CONV_WIDTH, cols] * qkv_scr[base:base + CHUNK, cols]
            for j in range(CONV_WIDTH - 1):
                off = base - (CONV_WIDTH - 1) + j
                acc = acc + conv_ref[j:j + 1, cols] * qkv_scr[off:off + CHUNK, cols]
            c = _silu(acc)
            if cb < DN_HEADS:
                c = c * (lax.rsqrt(jnp.sum(c * c, axis=-1, keepdims=True) + EPS) * (DN_HEAD_DIM ** -0.5))
                q_scr[r:r + CHUNK, cols] = c
            elif cb < 2 * DN_HEADS:
                c = c * lax.rsqrt(jnp.sum(c * c, axis=-1, keepdims=True) + EPS)
                k_scr[r:r + CHUNK, cb * LANES - DN_WIDTH:(cb + 1) * LANES - DN_WIDTH] = c
            else:
                v_scr[r:r + CHUNK, cb * LANES - 2 * DN_WIDTH:(cb + 1) * LANES - 2 * DN_WIDTH] = c
    qkv_scr[0:CONV_HDR, :] = qkv_scr[ts:ts + CONV_HDR, :]

    lane = lax.broadcasted_iota(jnp.int32, small.shape, 1)
    is_g = (lane >= DN_HEADS) & (lane < 2 * DN_HEADS)
    neg_a = dnvec_ref[0:1, :]
    dt_b = dnvec_ref[1:2, :]
    gates = jnp.where(is_g, neg_a * _softplus(small + dt_b), _sigmoid(small))
    g_hi, g_mid = _split2(gates)
    gcum = _dot(tri_ref[...], g_hi) + _dot(tri_ref[...], g_mid)
    bb_scr[...] = _dot(g_hi, eselb_ref[...])
    half_w = SSM_WIDTH // 2
    zs_scr[:, :half_w] = _dot(h, wzs_ref[:, :half_w])
    c_hi, c_mid = _split2(gcum)
    gb_scr[...] = _dot(c_hi, eselg_ref[...]) + _dot(c_mid, eselg_ref[...])
    zs_scr[:, half_w:] = _dot(h, wzs_ref[:, half_w:])

    xg = [[None, None] for _ in range(SSM_GROUPS)]
    for vcol in range(SSM_WIDTH // LANES):
        for jh in range(2):
            outs = _piece_transpose8(
                [u_scr[vcol, pl.ds(jh * 8 + j8, nblk, stride=SSM_L), :] for j8 in range(8)])
            for g8 in range(8):
                xg[vcol * 8 + g8][jh] = outs[g8]
    for g in range(SSM_GROUPS):
        xg_scr[g] = jnp.concatenate(xg[g], axis=1).astype(BF16)

    def ssm_state_in(pr):
        xp = jnp.concatenate([xg_scr[2 * pr], xg_scr[2 * pr + 1]], axis=1)
        bpair = _dot(xp, bpow_ref[pr])
        bre_scr[:, pr * LANES:(pr + 1) * LANES] = bpair[:, :LANES]
        bim_scr[:, pr * LANES:(pr + 1) * LANES] = bpair[:, LANES:]

    def ssm_state_out(pr):
        sp = jnp.concatenate([sre_scr[:, pr * LANES:(pr + 1) * LANES],
                              sim_scr[:, pr * LANES:(pr + 1) * LANES]], axis=1).astype(BF16)
        ys = _dot(sp, cpow_ref[pr])
        for k in range(2):
            g = 2 * pr + k
            yg_scr[g] = ys[:, k * SSM_ROW:(k + 1) * SSM_ROW] + _dot(xg_scr[g], mssm_ref[g])

    lane_c = lax.broadcasted_iota(jnp.int32, (CHUNK, LANES), 1)
    low_half = lane_c < CHUNK

    def blockdiag(xc):
        xb = xc.astype(BF16)
        zero = jnp.zeros((CHUNK, LANES), BF16)
        x01, x23 = xb[:, :LANES], xb[:, LANES:]
        return jnp.concatenate([
            jnp.concatenate([jnp.where(low_half, x01, zero), zero], axis=1),
            jnp.concatenate([jnp.where(low_half, zero, x01), zero], axis=1),
            jnp.concatenate([zero, jnp.where(low_half, x23, zero)], axis=1),
            jnp.concatenate([zero, jnp.where(low_half, zero, x23)], axis=1)], axis=0)

    def headcat(m):
        return jnp.concatenate(
            [jnp.where(low_half, m[0:CHUNK, :LANES], m[CHUNK:2 * CHUNK, :LANES]),
             jnp.where(low_half, m[2 * CHUNK:3 * CHUNK, LANES:], m[3 * CHUNK:4 * CHUNK, LANES:])], axis=1)

    def chunk_setup(ci):
        rows = slice(ci * CHUNK, (ci + 1) * CHUNK)

        def stack(ref):
            return jnp.concatenate(
                [ref[rows, hd * DN_HEAD_DIM:(hd + 1) * DN_HEAD_DIM] for hd in range(DN_HEADS)], axis=0)

        kst = stack(k_scr)
        qst = stack(q_scr)
        vst = stack(v_scr)
        gcol = stack(gb_scr)
        bcol = stack(bb_scr)
        kb = kst * bcol
        vb = vst * bcol
        egc = jnp.exp(gcol)
        grow = jnp.transpose(gcol)[0:1, :]
        gcol_cat = jnp.concatenate(
            [jnp.where(low_half, gb_scr[rows, 0:LANES], gb_scr[rows, LANES:2 * LANES]),
             jnp.where(low_half, gb_scr[rows, 2 * LANES:3 * LANES], gb_scr[rows, 3 * LANES:4 * LANES])],
            axis=1)
        causal = causalcat_ref[...]
        decay = jnp.exp(jnp.where(causal > 0.0, gcol_cat - grow, 0.0)) * causal
        kk = _dot_nt(jnp.concatenate([kb, qst], axis=0).astype(BF16), kst.astype(BF16))
        a_cat = headcat(kk[:HROWS]) * decay * strictcat_ref[...]
        intra_scr[ci] = (headcat(kk[HROWS:]) * decay).astype(BF16)
        rhs = jnp.concatenate([vb, kb * egc], axis=1).astype(BF16)
        qg = (qst * egc).astype(BF16)
        kdecs = []
        for hd in range(DN_HEADS):
            hs = slice(hd * CHUNK, (hd + 1) * CHUNK)
            g_h = gcol[hs]
            g_last = g_h[CHUNK - 1:CHUNK, :]
            kdecs.append(kst[hs] * jnp.exp(g_last - g_h))
            elast_scr[ci, hd:hd + 1, :] = jnp.exp(g_last)
        kdec_scr[ci] = jnp.concatenate(kdecs, axis=0).astype(BF16)
        return a_cat, rhs, qg

    def delta_setup_steps():
        for g0 in range(0, nchunk, GDN_GROUP):
            group = range(g0, min(g0 + GDN_GROUP, nchunk))
            setup = []
            for ci in group:
                setup.append(chunk_setup(ci))
                yield
            t_cats = [eyecat_ref[...] - a_cat for a_cat, _, _ in setup]
            p_cats = []
            for a_cat, _, _ in setup:
                p_cats.append(_dot(a_cat.astype(BF16), blockdiag(a_cat)))
            yield
            for lvl in range(5):
                for k in range(len(setup)):
                    p_bd = blockdiag(p_cats[k])
                    if lvl < 4:
                        both = _dot(jnp.concatenate([t_cats[k], p_cats[k]], axis=0).astype(BF16), p_bd)
                        t_cats[k] = t_cats[k] + both[:CHUNK]
                        p_cats[k] = both[CHUNK:]
                    else:
                        t_cats[k] = t_cats[k] + _dot(t_cats[k].astype(BF16), p_bd)
                yield
            for k, ci in enumerate(group):
                _, rhs, qg = setup[k]
                sol = _dot(blockdiag(t_cats[k]), rhs)
                u1_scr[ci] = sol[:, :DN_HEAD_DIM]
                w_mat = sol[:, DN_HEAD_DIM:].astype(BF16)
                for hd in range(DN_HEADS):
                    hs = slice(hd * CHUNK, (hd + 1) * CHUNK)
                    wq_scr[ci * DN_HEADS + hd] = jnp.concatenate([w_mat[hs], qg[hs]], axis=0)
            yield

    def side_projection_steps():
        for pr in range(SSM_PAIRS):
            ssm_state_in(pr)
            yield

    _weave(delta_setup_steps(), side_projection_steps())

    lre = lam_ref[0:1, :]
    lim = lam_ref[1:2, :]
    sre = carry_scr[0:1, :]
    sim = carry_scr[1:2, :]
    for cb in range(nblk):
        sre_scr[cb:cb + 1, :] = sre
        sim_scr[cb:cb + 1, :] = sim
        bre = bre_scr[cb:cb + 1, :]
        bim = bim_scr[cb:cb + 1, :]
        sre, sim = lre * sre - lim * sim + bre, lre * sim + lim * sre + bim
    carry_scr[0:1, :] = sre
    carry_scr[1:2, :] = sim

    def delta_recurrence_steps():
        for ci in range(nchunk):
            rows = slice(ci * CHUNK, (ci + 1) * CHUNK)
            u_mat = u1_scr[ci]
            vnews = []
            oinit = []
            for hd in range(DN_HEADS):
                hs = slice(hd * CHUNK, (hd + 1) * CHUNK)
                ws = _dot(wq_scr[ci * DN_HEADS + hd], state_scr[hd].astype(BF16))
                vnews.append(u_mat[hs] - ws[:CHUNK])
                oinit.append(ws[CHUNK:])
            vnew_b = jnp.concatenate(vnews, axis=0).astype(BF16)
            yield
            o_st = jnp.concatenate(oinit, axis=0) + _dot(blockdiag(intra_scr[ci]), vnew_b)
            kdec = kdec_scr[ci]
            for hd in range(DN_HEADS):
                hs = slice(hd * CHUNK, (hd + 1) * CHUNK)
                state_scr[hd] = (state_scr[hd] * elast_scr[ci, hd:hd + 1, :]
                                 + _dot_tn(kdec[hs], vnew_b[hs]))
                odn_scr[rows, hd * DN_HEAD_DIM:(hd + 1) * DN_HEAD_DIM] = o_st[hs]
            yield

    def ssm_output_steps():
        for pr in range(SSM_PAIRS):
            ssm_state_out(pr)
            yield
        yg = [yg_scr[g] for g in range(SSM_GROUPS)]
        for vcol in range(SSM_WIDTH // LANES):
            for ih in range(2):
                outs = _piece_transpose8(
                    [yg[vcol * 8 + g8][:, ih * LANES:(ih + 1) * LANES] for g8 in range(8)])
                for i8 in range(8):
                    y_scr[vcol, pl.ds(ih * 8 + i8, nblk, stride=SSM_L), :] = outs[i8]
        for rows in row_halves:
            y_val = jnp.concatenate([y_scr[vcol, rows, :] for vcol in range(SSM_WIDTH // LANES)], axis=1)
            ge = _gelu_tanh(y_val).astype(BF16)
            gl = _dot(ge, gluw_ref[...]) + glub_ref[...]
            ossm_scr[rows, :] = (gl[:, :SSM_WIDTH] * _sigmoid(gl[:, SSM_WIDTH:])
                                 * _silu(zs_scr[rows, :])).astype(BF16)
            yield

    _weave(delta_recurrence_steps(), ssm_output_steps())

    for rows in row_halves:
        o_parts = []
        for hd in range(DN_HEADS):
            sl = slice(hd * DN_HEAD_DIM, (hd + 1) * DN_HEAD_DIM)
            oh = odn_scr[rows, sl]
            on = oh * lax.rsqrt(jnp.mean(oh * oh, axis=-1, keepdims=True) + EPS) * normg_ref[...]
            o_parts.append((on * _silu(zdn_scr[rows, sl])).astype(BF16))
        mix_in = jnp.concatenate(o_parts + [ossm_scr[rows, :]], axis=1)
        mix = _dot(mix_in, wout_ref[...])
        mn = mix * lax.rsqrt(jnp.mean(mix * mix, axis=-1, keepdims=True) + EPS) * post_g_ref[...]
        o_ref[0, rows, :] = x_ref[0, rows, :] + mn


def _ssm_matrices(a_re, a_im, b_re, b_im, c_re, c_im, d, log_dt):
    hp = lax.Precision.HIGHEST
    dt = jnp.exp(log_dt.astype(F32))[:, None]
    a_re = a_re.astype(F32)
    a_im = a_im.astype(F32)
    mag = jnp.exp(a_re * dt)
    ang = a_im * dt
    lb_re, lb_im = mag * jnp.cos(ang), mag * jnp.sin(ang)
    den = a_re * a_re + a_im * a_im
    num_re, num_im = lb_re - 1.0, lb_im
    coef_re = (num_re * a_re + num_im * a_im) / den
    coef_im = (num_im * a_re - num_re * a_im) / den
    b_re = b_re.astype(F32)
    b_im = b_im.astype(F32)
    bb_re = coef_re[..., None] * b_re - coef_im[..., None] * b_im
    bb_im = coef_re[..., None] * b_im + coef_im[..., None] * b_re
    taus = jnp.arange(SSM_L + 1, dtype=F32)[:, None, None]
    pmag = jnp.exp(a_re * dt * taus)
    pang = a_im * dt * taus
    lp_re, lp_im = pmag * jnp.cos(pang), pmag * jnp.sin(pang)
    c_re = c_re.astype(F32)
    c_im = c_im.astype(F32)
    cl_re = c_re[None] * lp_re[:, :, None, :] - c_im[None] * lp_im[:, :, None, :]
    cl_im = c_re[None] * lp_im[:, :, None, :] + c_im[None] * lp_re[:, :, None, :]
    kern = (jnp.einsum("tghp,gpk->tghk", cl_re[:SSM_L], bb_re, precision=hp)
            - jnp.einsum("tghp,gpk->tghk", cl_im[:SSM_L], bb_im, precision=hp))
    kern = kern.at[0].add(d.astype(F32).reshape(SSM_GROUPS, SSM_GROUP)[:, :, None]
                          * jnp.eye(SSM_GROUP, dtype=F32)[None])
    ii = jnp.arange(SSM_L)
    lag = ii[None, :] - ii[:, None]
    kt = kern[jnp.clip(lag, 0, SSM_L - 1)]
    kt = jnp.where((lag >= 0)[:, :, None, None, None], kt, 0.0)
    mssm = jnp.transpose(kt, (2, 0, 4, 1, 3)).reshape(SSM_GROUPS, SSM_ROW, SSM_ROW)
    lr = lp_re[:SSM_L][::-1]
    li = lp_im[:SSM_L][::-1]
    bp_re = lr[:, :, :, None] * bb_re[None] - li[:, :, :, None] * bb_im[None]
    bp_im = lr[:, :, :, None] * bb_im[None] + li[:, :, :, None] * bb_re[None]
    bp_re = jnp.transpose(bp_re, (1, 0, 3, 2)).reshape(SSM_GROUPS, SSM_ROW, SSM_STATE)
    bp_im = jnp.transpose(bp_im, (1, 0, 3, 2)).reshape(SSM_GROUPS, SSM_ROW, SSM_STATE)
    zb = jnp.zeros_like(bp_re)
    ev, od = slice(0, None, 2), slice(1, None, 2)
    bpow = jnp.concatenate([
        jnp.concatenate([bp_re[ev], zb[ev], bp_im[ev], zb[ev]], axis=2),
        jnp.concatenate([zb[od], bp_re[od], zb[od], bp_im[od]], axis=2)], axis=1)
    co_re = jnp.transpose(cl_re[1:], (1, 3, 0, 2)).reshape(SSM_GROUPS, SSM_STATE, SSM_ROW)
    co_im = -jnp.transpose(cl_im[1:], (1, 3, 0, 2)).reshape(SSM_GROUPS, SSM_STATE, SSM_ROW)
    zc = jnp.zeros_like(co_re)
    cpow = jnp.concatenate([
        jnp.concatenate([co_re[ev], zc[ev]], axis=2),
        jnp.concatenate([zc[od], co_re[od]], axis=2),
        jnp.concatenate([co_im[ev], zc[ev]], axis=2),
        jnp.concatenate([zc[od], co_im[od]], axis=2)], axis=1)
    lam = jnp.stack([lp_re[SSM_L].reshape(-1), lp_im[SSM_L].reshape(-1)], axis=0)
    return mssm.astype(BF16), bpow.astype(BF16), cpow.astype(BF16), lam


def _constants(ts):
    r = jnp.arange(ts)
    tri = ((r[:, None] // CHUNK == r[None, :] // CHUNK) & (r[:, None] >= r[None, :])).astype(BF16)
    lane = jnp.arange(LANES)
    col = jnp.arange(DN_WIDTH)
    eselb = (lane[:, None] == col[None, :] // DN_HEAD_DIM).astype(BF16)
    eselg = (lane[:, None] == DN_HEADS + col[None, :] // DN_HEAD_DIM).astype(BF16)
    ri = jnp.arange(CHUNK)[:, None]
    cj = jnp.arange(HROWS)[None, :] % CHUNK
    causalcat = (ri >= cj).astype(F32)
    strictcat = (ri > cj).astype(F32)
    eyecat = (ri == cj).astype(F32)
    return tri, eselg, eselb, causalcat, strictcat, eyecat


def _layer(x, ts, pre_g, post_g, w_in, conv_w, a_log, dt_bias, norm_g, ssm, glu_w, glu_b, w_out):
    bsz, s, d = x.shape
    c_qkv = 3 * DN_WIDTH
    c_z = c_qkv + DN_WIDTH
    c_beta = c_z + DN_HEADS
    c_alpha = c_beta + DN_HEADS
    c_u = c_alpha + SSM_WIDTH
    wqkv = w_in[:, :c_qkv].astype(BF16)
    wz = w_in[:, c_qkv:c_z].astype(BF16)
    wu = w_in[:, c_alpha:c_u].astype(BF16)
    wzs = w_in[:, c_u:].astype(BF16)
    wsm = jnp.pad(w_in[:, c_z:c_alpha], ((0, 0), (0, LANES - 2 * DN_HEADS))).astype(BF16)
    dnvec = jnp.zeros((2, LANES), F32)
    dnvec = dnvec.at[0, DN_HEADS:2 * DN_HEADS].set(-jnp.exp(a_log.astype(F32)))
    dnvec = dnvec.at[1, DN_HEADS:2 * DN_HEADS].set(dt_bias.astype(F32))
    mssm, bpow, cpow, lam = ssm
    consts = _constants(ts)
    nblk = ts // SSM_L
    nstate = SSM_GROUPS * SSM_STATE

    def full(a):
        nd = a.ndim
        return pl.BlockSpec(a.shape, lambda b, t, _nd=nd: (0,) * _nd)

    operands = [
        pre_g.reshape(1, d).astype(F32), post_g.reshape(1, d).astype(F32),
        wqkv, wz, wu, wzs, wsm, conv_w.astype(F32), dnvec, norm_g.reshape(1, DN_HEAD_DIM).astype(F32),
        *consts, mssm, bpow, cpow, lam,
        glu_w.astype(BF16), glu_b.reshape(1, -1).astype(F32), w_out.astype(BF16)]
    scratch = [
        pltpu.VMEM((ts, d), BF16),
        pltpu.VMEM((ts + CONV_HDR, c_qkv), F32),
        pltpu.VMEM((ts, DN_WIDTH), F32),
        pltpu.VMEM((ts, DN_WIDTH), F32),
        pltpu.VMEM((ts, DN_WIDTH), F32),
        pltpu.VMEM((ts, DN_WIDTH), F32),
        pltpu.VMEM((ts, DN_WIDTH), F32),
        pltpu.VMEM((ts, DN_WIDTH), F32),
        pltpu.VMEM((DN_HEADS, DN_HEAD_DIM, DN_HEAD_DIM), F32),
        pltpu.VMEM((ts // CHUNK, HROWS, DN_HEAD_DIM), F32),
        pltpu.VMEM((ts // CHUNK * DN_HEADS, 2 * CHUNK, DN_HEAD_DIM), BF16),
        pltpu.VMEM((ts // CHUNK, CHUNK, HROWS), BF16),
        pltpu.VMEM((ts // CHUNK, HROWS, DN_HEAD_DIM), BF16),
        pltpu.VMEM((ts // CHUNK, 8, DN_HEAD_DIM), F32),
        pltpu.VMEM((SSM_WIDTH // LANES, ts, LANES), F32),
        pltpu.VMEM((SSM_WIDTH // LANES, ts, LANES), F32),
        pltpu.VMEM((SSM_GROUPS, nblk, SSM_ROW), BF16),
        pltpu.VMEM((SSM_GROUPS, nblk, SSM_ROW), F32),
        pltpu.VMEM((ts, DN_WIDTH), F32),
        pltpu.VMEM((ts, SSM_WIDTH), F32),
        pltpu.VMEM((ts, SSM_WIDTH), BF16),
        pltpu.VMEM((nblk, nstate), F32),
        pltpu.VMEM((nblk, nstate), F32),
        pltpu.VMEM((nblk, nstate), F32),
        pltpu.VMEM((nblk, nstate), F32),
        pltpu.VMEM((2, nstate), F32),
    ]
    return pl.pallas_call(
        functools.partial(_layer_kernel, ts=ts),
        out_shape=jax.ShapeDtypeStruct(x.shape, x.dtype),
        grid=(bsz, s // ts),
        in_specs=[pl.BlockSpec((1, ts, d), lambda b, t: (b, t, 0))] + [full(a) for a in operands],
        out_specs=pl.BlockSpec((1, ts, d), lambda b, t: (b, t, 0)),
        scratch_shapes=scratch,
        compiler_params=pltpu.CompilerParams(
            dimension_semantics=("arbitrary", "arbitrary"),
            vmem_limit_bytes=VMEM_LIMIT_BYTES),
        name="hybrid_layer",
    )(x, *operands)


def _pick_tile(s):
    for ts in (512, 256, 128, 64):
        if s % ts == 0:
            return ts
    raise ValueError(f"sequence length {s} must be a multiple of {CHUNK}")


def kernel(x, pre_norm_g, post_norm_g, w_in, conv_w, dn_a_log, dn_dt_bias, dn_norm_g, ssm_a_re, ssm_a_im, ssm_b_re, ssm_b_im, ssm_c_re, ssm_c_im, ssm_d, ssm_log_dt, glu_w, glu_b, w_out):
    depth = w_in.shape[0]
    ts = _pick_tile(x.shape[1])
    for l in range(depth):
        ssm = _ssm_matrices(ssm_a_re[l], ssm_a_im[l], ssm_b_re[l], ssm_b_im[l], ssm_c_re[l],
                            ssm_c_im[l], ssm_d[l], ssm_log_dt[l])
        x = _layer(x, ts, pre_norm_g[l], post_norm_g[l], w_in[l], conv_w[l], dn_a_log[l],
                   dn_dt_bias[l], dn_norm_g[l], ssm, glu_w[l], glu_b[l], w_out[l])
    return x
```

```python
import functools
import math

import jax
import jax.numpy as jnp
import numpy as np
from jax import lax
from jax.experimental import pallas as pl
from jax.experimental.pallas import tpu as pltpu

F32 = jnp.float32
BF16 = jnp.bfloat16

DN_HEADS = 4
DN_HEAD_DIM = 128
DN_WIDTH = DN_HEADS * DN_HEAD_DIM
CONV_WIDTH = 4
CHUNK = 64
HROWS = DN_HEADS * CHUNK
SSM_GROUP = 16
SSM_GROUPS = 32
SSM_STATE = 64
SSM_WIDTH = SSM_GROUP * SSM_GROUPS
SSM_L = 8
SSM_PAIRS = SSM_GROUPS // 2
SSM_ROW = SSM_L * SSM_GROUP
PAIR_ROW = 2 * SSM_ROW
EPS = 1e-6
NEG_LOG2E = -1.4426950408889634
LANES = 128
NORM_ROWS = 32
GDN_GROUP = 8
CONV_HDR = 8
VMEM_LIMIT_BYTES = 56 * 1024 * 1024


def _dot(a, b):
    return jnp.dot(a, b, preferred_element_type=F32)


def _dot_nt(a, b):
    return lax.dot_general(a, b, (((1,), (1,)), ((), ())), preferred_element_type=F32)


def _dot_tn(a, b):
    return lax.dot_general(a, b, (((0,), (0,)), ((), ())), preferred_element_type=F32)


def _split2(a):
    hi = a.astype(BF16)
    mid = (a - hi.astype(F32)).astype(BF16)
    return hi, mid


def _sigmoid(x):
    return 1.0 / (1.0 + jnp.exp2(x * NEG_LOG2E))


def _silu(x):
    return x * _sigmoid(x)


def _softplus(x):
    return jnp.maximum(x, 0.0) + jnp.log(1.0 + jnp.exp(-jnp.abs(x)))


def _gelu_tanh(x):
    c = math.sqrt(2.0 / math.pi)
    return 0.5 * x * (1.0 + jnp.tanh(c * (x + 0.044715 * (x * x * x))))


def _piece_transpose8(vs):
    lane = lax.broadcasted_iota(jnp.int32, vs[0].shape, 1)
    piece = lane // SSM_GROUP
    for d in (4, 2, 1):
        keep = (piece & d) == 0
        new = list(vs)
        for a in range(8):
            if a & d == 0:
                b = a + d
                new[a] = jnp.where(keep, vs[a], pltpu.roll(vs[b], d * SSM_GROUP, 1))
                new[b] = jnp.where(keep, pltpu.roll(vs[a], LANES - d * SSM_GROUP, 1), vs[b])
        vs = new
    return vs


_DONE = object()


def _weave(main, side):
    side_done = False
    for _ in main:
        if not side_done:
            side_done = next(side, _DONE) is _DONE
    for _ in side:
        pass


def _layer_kernel(x_ref, pre_g_ref, post_g_ref, wqkv_ref, wz_ref, wu_ref, wzs_ref, wsm_ref,
                  conv_ref, dnvec_ref, normg_ref, tri_ref, eselg_ref, eselb_ref, causalcat_ref, strictcat_ref,
                  eyecat_ref, mssm_ref, bpow_ref, cpow_ref, lam_ref, gluw_ref, glub_ref, wout_ref,
                  o_ref,
                  h_scr, qkv_scr, q_scr, k_scr, v_scr, gb_scr, bb_scr, odn_scr, state_scr,
                  oprime_scr, gq_scr, hmat_scr, elast_scr,
                  u_scr, y_scr, xp_scr, yp_scr, zdn_scr, zs_scr, ossm_scr,
                  bre_scr, bim_scr, sre_scr, sim_scr, carry_scr, *, ts):
    t = pl.program_id(1)
    nchunk = ts // CHUNK
    nblk = ts // SSM_L

    @pl.when(t == 0)
    def _():
        qkv_scr[0:CONV_HDR, :] = jnp.zeros((CONV_HDR, 3 * DN_WIDTH), F32)
        state_scr[...] = jnp.zeros(state_scr.shape, F32)
        carry_scr[...] = jnp.zeros(carry_scr.shape, F32)

    for r in range(0, ts, NORM_ROWS):
        xb = x_ref[0, r:r + NORM_ROWS, :]
        ms = jnp.mean(xb * xb, axis=-1, keepdims=True)
        h_scr[r:r + NORM_ROWS, :] = (xb * lax.rsqrt(ms + EPS) * pre_g_ref[...]).astype(BF16)
    h = h_scr[...]

    small = _dot(h, wsm_ref[...])
    qkv_scr[CONV_HDR:CONV_HDR + ts, :] = _dot(h, wqkv_ref[...])
    u_val = _dot(h, wu_ref[...])
    for vcol in range(SSM_WIDTH // LANES):
        u_scr[vcol] = u_val[:, vcol * LANES:(vcol + 1) * LANES]
    zdn_scr[...] = _dot(h, wz_ref[...])

    for cb in range(3 * DN_WIDTH // LANES):
        cols = slice(cb * LANES, (cb + 1) * LANES)
        for r in range(0, ts, CHUNK):
            base = CONV_HDR + r
            acc = conv_ref[CONV_WIDTH - 1:CONV_WIDTH, cols] * qkv_scr[base:base + CHUNK, cols]
            for j in range(CONV_WIDTH - 1):
                off = base - (CONV_WIDTH - 1) + j
                acc = acc + conv_ref[j:j + 1, cols] * qkv_scr[off:off + CHUNK, cols]
            c = _silu(acc)
            if cb < DN_HEADS:
                c = c * (lax.rsqrt(jnp.sum(c * c, axis=-1, keepdims=True) + EPS) * (DN_HEAD_DIM ** -0.5))
                q_scr[r:r + CHUNK, cols] = c
            elif cb < 2 * DN_HEADS:
                c = c * lax.rsqrt(jnp.sum(c * c, axis=-1, keepdims=True) + EPS)
                k_scr[r:r + CHUNK, cb * LANES - DN_WIDTH:(cb + 1) * LANES - DN_WIDTH] = c
            else:
                v_scr[r:r + CHUNK, cb * LANES - 2 * DN_WIDTH:(cb + 1) * LANES - 2 * DN_WIDTH] = c
    qkv_scr[0:CONV_HDR, :] = qkv_scr[ts:ts + CONV_HDR, :]

    lane = lax.broadcasted_iota(jnp.int32, small.shape, 1)
    is_g = (lane >= DN_HEADS) & (lane < 2 * DN_HEADS)
    neg_a = dnvec_ref[0:1, :]
    dt_b = dnvec_ref[1:2, :]
    gates = jnp.where(is_g, neg_a * _softplus(small + dt_b), _sigmoid(small))
    g_hi, g_mid = _split2(gates)
    gcum = _dot(tri_ref[...], g_hi) + _dot(tri_ref[...], g_mid)
    bb_scr[...] = _dot(g_hi, eselb_ref[...])
    half_w = SSM_WIDTH // 2
    zs_scr[:, :half_w] = _dot(h, wzs_ref[:, :half_w])
    c_hi, c_mid = _split2(gcum)
    gb_scr[...] = _dot(c_hi, eselg_ref[...]) + _dot(c_mid, eselg_ref[...])
    zs_scr[:, half_w:] = _dot(h, wzs_ref[:, half_w:])

    for vcol in range(SSM_WIDTH // LANES):
        outs = _piece_transpose8(
            [u_scr[vcol, pl.ds(j8, nblk, stride=SSM_L), :] for j8 in range(SSM_L)])
        for g8 in range(8):
            g = vcol * 8 + g8
            xp_scr[g // 2, :, (g % 2) * SSM_ROW:(g % 2 + 1) * SSM_ROW] = outs[g8].astype(BF16)

    def ssm_state_in(pr):
        bpair = _dot(xp_scr[pr], bpow_ref[pr])
        bre_scr[:, pr * LANES:(pr + 1) * LANES] = bpair[:, :LANES]
        bim_scr[:, pr * LANES:(pr + 1) * LANES] = bpair[:, LANES:]

    def ssm_state_out(pr):
        sp = jnp.concatenate([sre_scr[:, pr * LANES:(pr + 1) * LANES],
                              sim_scr[:, pr * LANES:(pr + 1) * LANES]], axis=1).astype(BF16)
        yp_scr[pr] = _dot(sp, cpow_ref[pr]) + _dot(xp_scr[pr], mssm_ref[pr])

    lane_c = lax.broadcasted_iota(jnp.int32, (CHUNK, LANES), 1)
    low_half = lane_c < CHUNK

    def blockdiag(xc):
        xb = xc.astype(BF16)
        zero = jnp.zeros((CHUNK, LANES), BF16)
        x01, x23 = xb[:, :LANES], xb[:, LANES:]
        return jnp.concatenate([
            jnp.concatenate([jnp.where(low_half, x01, zero), zero], axis=1),
            jnp.concatenate([jnp.where(low_half, zero, x01), zero], axis=1),
            jnp.concatenate([zero, jnp.where(low_half, x23, zero)], axis=1),
            jnp.concatenate([zero, jnp.where(low_half, zero, x23)], axis=1)], axis=0)

    def headcat(m):
        return jnp.concatenate(
            [jnp.where(low_half, m[0:CHUNK, :LANES], m[CHUNK:2 * CHUNK, :LANES]),
             jnp.where(low_half, m[2 * CHUNK:3 * CHUNK, LANES:], m[3 * CHUNK:4 * CHUNK, LANES:])], axis=1)

    def chunk_setup(ci):
        rows = slice(ci * CHUNK, (ci + 1) * CHUNK)

        def stack(ref):
            return jnp.concatenate(
                [ref[rows, hd * DN_HEAD_DIM:(hd + 1) * DN_HEAD_DIM] for hd in range(DN_HEADS)], axis=0)

        kst = stack(k_scr)
        qst = stack(q_scr)
        vst = stack(v_scr)
        gcol = stack(gb_scr)
        bcol = stack(bb_scr)
        kb = kst * bcol
        vb = vst * bcol
        egc = jnp.exp(gcol)
        grow = jnp.transpose(gcol)[0:1, :]
        gcol_cat = jnp.concatenate(
            [jnp.where(low_half, gb_scr[rows, 0:LANES], gb_scr[rows, LANES:2 * LANES]),
             jnp.where(low_half, gb_scr[rows, 2 * LANES:3 * LANES], gb_scr[rows, 3 * LANES:4 * LANES])],
            axis=1)
        causal = causalcat_ref[...]
        decay = jnp.exp(jnp.where(causal > 0.0, gcol_cat - grow, 0.0)) * causal
        kk = _dot_nt(jnp.concatenate([kb, qst], axis=0).astype(BF16), kst.astype(BF16))
        a_cat = headcat(kk[:HROWS]) * decay * strictcat_ref[...]
        intra_bd = blockdiag(headcat(kk[HROWS:]) * decay)
        rhs = jnp.concatenate([vb, kb * egc], axis=1).astype(BF16)
        qg = qst * egc
        kdecs = []
        for hd in range(DN_HEADS):
            hs = slice(hd * CHUNK, (hd + 1) * CHUNK)
            g_h = gcol[hs]
            g_last = g_h[CHUNK - 1:CHUNK, :]
            kdecs.append((kst[hs] * jnp.exp(g_last - g_h)).astype(BF16))
            elast_scr[ci, hd:hd + 1, :] = jnp.exp(g_last)
        return a_cat, rhs, qg, intra_bd, kdecs

    def delta_setup_steps():
        for g0 in range(0, nchunk, GDN_GROUP):
            group = range(g0, min(g0 + GDN_GROUP, nchunk))
            setup = []
            for ci in group:
                setup.append(chunk_setup(ci))
                yield
            t_cats = [eyecat_ref[...] - st[0] for st in setup]
            p_cats = []
            for st in setup:
                p_cats.append(_dot(st[0].astype(BF16), blockdiag(st[0])))
            yield
            for lvl in range(5):
                for k in range(len(setup)):
                    p_bd = blockdiag(p_cats[k])
                    if lvl < 4:
                        both = _dot(jnp.concatenate([t_cats[k], p_cats[k]], axis=0).astype(BF16), p_bd)
                        t_cats[k] = t_cats[k] + both[:CHUNK]
                        p_cats[k] = both[CHUNK:]
                    else:
                        t_cats[k] = t_cats[k] + _dot(t_cats[k].astype(BF16), p_bd)
                yield
            uw = []
            for k in range(len(setup)):
                sol = _dot(blockdiag(t_cats[k]), setup[k][1])
                uw.append(sol.astype(BF16))
            yield
            for k, ci in enumerate(group):
                _, _, qg, intra_bd, kdecs = setup[k]
                iuw = _dot(intra_bd, uw[k])
                oprime_scr[ci] = iuw[:, :DN_HEAD_DIM]
                qprime = (qg - iuw[:, DN_HEAD_DIM:]).astype(BF16)
                for hd in range(DN_HEADS):
                    hs = slice(hd * CHUNK, (hd + 1) * CHUNK)
                    kuw = _dot_tn(kdecs[hd], uw[k][hs])
                    hmat_scr[ci * DN_HEADS + hd] = kuw[:, :DN_HEAD_DIM]
                    gq_scr[ci * DN_HEADS + hd] = jnp.concatenate(
                        [(-kuw[:, DN_HEAD_DIM:]).astype(BF16), qprime[hs]], axis=0)
                yield

    def side_projection_steps():
        for pr in range(SSM_PAIRS):
            ssm_state_in(pr)
            yield

    _weave(delta_setup_steps(), side_projection_steps())

    lre = lam_ref[0:1, :]
    lim = lam_ref[1:2, :]
    sre = carry_scr[0:1, :]
    sim = carry_scr[1:2, :]
    for cb in range(nblk):
        sre_scr[cb:cb + 1, :] = sre
        sim_scr[cb:cb + 1, :] = sim
        bre = bre_scr[cb:cb + 1, :]
        bim = bim_scr[cb:cb + 1, :]
        sre, sim = lre * sre - lim * sim + bre, lre * sim + lim * sre + bim
    carry_scr[0:1, :] = sre
    carry_scr[1:2, :] = sim

    def delta_recurrence_steps():
        for ci in range(nchunk):
            rows = slice(ci * CHUNK, (ci + 1) * CHUNK)
            oprime = oprime_scr[ci]
            for hd in range(DN_HEADS):
                hs = slice(hd * CHUNK, (hd + 1) * CHUNK)
                idx = ci * DN_HEADS + hd
                s_old = state_scr[hd]
                gs = _dot(gq_scr[idx], s_old.astype(BF16))
                state_scr[hd] = s_old * elast_scr[ci, hd:hd + 1, :] + gs[:DN_HEAD_DIM] + hmat_scr[idx]
                odn_scr[rows, hd * DN_HEAD_DIM:(hd + 1) * DN_HEAD_DIM] = gs[DN_HEAD_DIM:] + oprime[hs]
            yield

    def ssm_output_steps():
        for pr in range(SSM_PAIRS):
            ssm_state_out(pr)
            yield
        for vcol in range(SSM_WIDTH // LANES):
            outs = _piece_transpose8(
                [yp_scr[(vcol * 8 + g8) // 2, :, (g8 % 2) * SSM_ROW:(g8 % 2 + 1) * SSM_ROW] for g8 in range(8)])
            for i8 in range(SSM_L):
                y_scr[vcol, pl.ds(i8, nblk, stride=SSM_L), :] = outs[i8]
        y_val = jnp.concatenate([y_scr[vcol] for vcol in range(SSM_WIDTH // LANES)], axis=1)
        ge = _gelu_tanh(y_val).astype(BF16)
        gl = _dot(ge, gluw_ref[...]) + glub_ref[...]
        ossm_scr[...] = (gl[:, :SSM_WIDTH] * _sigmoid(gl[:, SSM_WIDTH:]) * _silu(zs_scr[...])).astype(BF16)
        yield

    _weave(delta_recurrence_steps(), ssm_output_steps())

    o_parts = []
    for hd in range(DN_HEADS):
        sl = slice(hd * DN_HEAD_DIM, (hd + 1) * DN_HEAD_DIM)
        oh = odn_scr[:, sl]
        on = oh * lax.rsqrt(jnp.mean(oh * oh, axis=-1, keepdims=True) + EPS) * normg_ref[...]
        o_parts.append((on * _silu(zdn_scr[:, sl])).astype(BF16))
    mix_in = jnp.concatenate(o_parts + [ossm_scr[...]], axis=1)
    mix = _dot(mix_in, wout_ref[...])
    mn = mix * lax.rsqrt(jnp.mean(mix * mix, axis=-1, keepdims=True) + EPS) * post_g_ref[...]
    o_ref[0] = x_ref[0] + mn


def _ssm_matrices(a_re, a_im, b_re, b_im, c_re, c_im, d, log_dt):
    hp = lax.Precision.HIGHEST
    dt = jnp.exp(log_dt.astype(F32))[:, None]
    a_re = a_re.astype(F32)
    a_im = a_im.astype(F32)
    mag = jnp.exp(a_re * dt)
    ang = a_im * dt
    lb_re, lb_im = mag * jnp.cos(ang), mag * jnp.sin(ang)
    den = a_re * a_re + a_im * a_im
    num_re, num_im = lb_re - 1.0, lb_im
    coef_re = (num_re * a_re + num_im * a_im) / den
    coef_im = (num_im * a_re - num_re * a_im) / den
    b_re = jnp.swapaxes(b_re.astype(F32), 1, 2)
    b_im = jnp.swapaxes(b_im.astype(F32), 1, 2)
    bb_re = coef_re[:, None, :] * b_re - coef_im[:, None, :] * b_im
    bb_im = coef_re[:, None, :] * b_im + coef_im[:, None, :] * b_re
    taus = jnp.arange(SSM_L + 1, dtype=F32)[None, :, None]
    pmag = jnp.exp((a_re * dt)[:, None, :] * taus)
    pang = (a_im * dt)[:, None, :] * taus
    lp_re, lp_im = pmag * jnp.cos(pang), pmag * jnp.sin(pang)
    c_re = c_re.astype(F32)
    c_im = c_im.astype(F32)
    cl_re = c_re[:, None] * lp_re[:, :, None, :] - c_im[:, None] * lp_im[:, :, None, :]
    cl_im = c_re[:, None] * lp_im[:, :, None, :] + c_im[:, None] * lp_re[:, :, None, :]
    kern = (jnp.einsum("gthp,gkp->tghk", cl_re[:, :SSM_L], bb_re, precision=hp)
            - jnp.einsum("gthp,gkp->tghk", cl_im[:, :SSM_L], bb_im, precision=hp))
    dmat = d.astype(F32).reshape(SSM_GROUPS, SSM_GROUP)[:, :, None] * jnp.eye(SSM_GROUP, dtype=F32)[None]
    kern = kern + jnp.asarray(_LAG0)[:, None, None, None] * dmat[None]
    mg = jnp.einsum("tji,tghk->gjkih", jnp.asarray(_LAG_SELECT), kern,
                    precision=hp).reshape(SSM_GROUPS, SSM_ROW, SSM_ROW)
    lr = lp_re[:, SSM_L - 1::-1][:, :SSM_L]
    li = lp_im[:, SSM_L - 1::-1][:, :SSM_L]
    bp_re = (lr[:, :, None, :] * bb_re[:, None] - li[:, :, None, :] * bb_im[:, None]
             ).reshape(SSM_GROUPS, SSM_ROW, SSM_STATE)
    bp_im = (lr[:, :, None, :] * bb_im[:, None] + li[:, :, None, :] * bb_re[:, None]
             ).reshape(SSM_GROUPS, SSM_ROW, SSM_STATE)
    co_re = jnp.transpose(cl_re[:, 1:], (0, 3, 1, 2)).reshape(SSM_GROUPS, SSM_STATE, SSM_ROW)
    co_im = -jnp.transpose(cl_im[:, 1:], (0, 3, 1, 2)).reshape(SSM_GROUPS, SSM_STATE, SSM_ROW)

    def pair(x):
        return x.reshape((SSM_PAIRS, 2) + x.shape[1:])

    def blocks(rows):
        return jnp.concatenate([jnp.concatenate(r, axis=2) for r in rows], axis=1)

    mp, bre, bim, cre, cim = pair(mg), pair(bp_re), pair(bp_im), pair(co_re), pair(co_im)
    zm, zb, zc = jnp.zeros_like(mp[:, 0]), jnp.zeros_like(bre[:, 0]), jnp.zeros_like(cre[:, 0])
    mssm = blocks([[mp[:, 0], zm], [zm, mp[:, 1]]])
    bpow = blocks([[bre[:, 0], zb, bim[:, 0], zb], [zb, bre[:, 1], zb, bim[:, 1]]])
    cpow = blocks([[cre[:, 0], zc], [zc, cre[:, 1]], [cim[:, 0], zc], [zc, cim[:, 1]]])
    lam = jnp.stack([lp_re[:, SSM_L].reshape(-1), lp_im[:, SSM_L].reshape(-1)], axis=0)
    return mssm.astype(BF16), bpow.astype(BF16), cpow.astype(BF16), lam


def _lag_tables():
    tau, j, i = np.meshgrid(np.arange(SSM_L), np.arange(SSM_L), np.arange(SSM_L), indexing="ij")
    return (i - j == tau).astype(np.float32), (np.arange(SSM_L) == 0).astype(np.float32)


_LAG_SELECT, _LAG0 = _lag_tables()


def _constants(ts):
    r = np.arange(ts)
    tri = (r[:, None] // CHUNK == r[None, :] // CHUNK) & (r[:, None] >= r[None, :])
    lane = np.arange(LANES)
    col = np.arange(DN_WIDTH)
    eselb = lane[:, None] == col[None, :] // DN_HEAD_DIM
    eselg = lane[:, None] == DN_HEADS + col[None, :] // DN_HEAD_DIM
    ri = np.arange(CHUNK)[:, None]
    cj = np.arange(HROWS)[None, :] % CHUNK

    def bf(m):
        return jnp.asarray(m, dtype=BF16)

    def f32(m):
        return jnp.asarray(m, dtype=F32)

    return bf(tri), bf(eselg), bf(eselb), f32(ri >= cj), f32(ri > cj), f32(ri == cj)


def _split_w_in(w_in):
    c_qkv = 3 * DN_WIDTH
    c_z = c_qkv + DN_WIDTH
    c_alpha = c_z + 2 * DN_HEADS
    c_u = c_alpha + SSM_WIDTH
    wb = w_in.astype(BF16)
    pad = [(0, 0)] * (w_in.ndim - 1) + [(0, LANES - 2 * DN_HEADS)]
    return (wb[..., :c_qkv], wb[..., c_qkv:c_z], wb[..., c_alpha:c_u], wb[..., c_u:],
            jnp.pad(wb[..., c_z:c_alpha], pad))


def _layer(x, ts, pre_g, post_g, w_parts, conv_w, a_log, dt_bias, norm_g, ssm, glu_w, glu_b, w_out):
    bsz, s, d = x.shape
    c_qkv = 3 * DN_WIDTH
    wqkv, wz, wu, wzs, wsm = w_parts
    dnvec = jnp.pad(jnp.stack([-jnp.exp(a_log.astype(F32)), dt_bias.astype(F32)]),
                    ((0, 0), (DN_HEADS, LANES - 2 * DN_HEADS)))
    mssm, bpow, cpow, lam = ssm
    consts = _constants(ts)
    nblk = ts // SSM_L
    nstate = SSM_GROUPS * SSM_STATE

    def full(a):
        nd = a.ndim
        return pl.BlockSpec(a.shape, lambda b, t, _nd=nd: (0,) * _nd)

    operands = [
        pre_g.reshape(1, d).astype(F32), post_g.reshape(1, d).astype(F32),
        wqkv, wz, wu, wzs, wsm, conv_w.astype(F32), dnvec, norm_g.reshape(1, DN_HEAD_DIM).astype(F32),
        *consts, mssm, bpow, cpow, lam,
        glu_w, glu_b.reshape(1, -1).astype(F32), w_out]
    scratch = [
        pltpu.VMEM((ts, d), BF16),
        pltpu.VMEM((ts + CONV_HDR, c_qkv), F32),
        pltpu.VMEM((ts, DN_WIDTH), F32),
        pltpu.VMEM((ts, DN_WIDTH), F32),
        pltpu.VMEM((ts, DN_WIDTH), F32),
        pltpu.VMEM((ts, DN_WIDTH), F32),
        pltpu.VMEM((ts, DN_WIDTH), F32),
        pltpu.VMEM((ts, DN_WIDTH), F32),
        pltpu.VMEM((DN_HEADS, DN_HEAD_DIM, DN_HEAD_DIM), F32),
        pltpu.VMEM((ts // CHUNK, HROWS, DN_HEAD_DIM), F32),
        pltpu.VMEM((ts // CHUNK * DN_HEADS, DN_HEAD_DIM + CHUNK, DN_HEAD_DIM), BF16),
        pltpu.VMEM((ts // CHUNK * DN_HEADS, DN_HEAD_DIM, DN_HEAD_DIM), F32),
        pltpu.VMEM((ts // CHUNK, 8, DN_HEAD_DIM), F32),
        pltpu.VMEM((SSM_WIDTH // LANES, ts, LANES), F32),
        pltpu.VMEM((SSM_WIDTH // LANES, ts, LANES), F32),
        pltpu.VMEM((SSM_PAIRS, nblk, PAIR_ROW), BF16),
        pltpu.VMEM((SSM_PAIRS, nblk, PAIR_ROW), F32),
        pltpu.VMEM((ts, DN_WIDTH), F32),
        pltpu.VMEM((ts, SSM_WIDTH), F32),
        pltpu.VMEM((ts, SSM_WIDTH), BF16),
        pltpu.VMEM((nblk, nstate), F32),
        pltpu.VMEM((nblk, nstate), F32),
        pltpu.VMEM((nblk, nstate), F32),
        pltpu.VMEM((nblk, nstate), F32),
        pltpu.VMEM((2, nstate), F32),
    ]
    return pl.pallas_call(
        functools.partial(_layer_kernel, ts=ts),
        out_shape=jax.ShapeDtypeStruct(x.shape, x.dtype),
        grid=(bsz, s // ts),
        in_specs=[pl.BlockSpec((1, ts, d), lambda b, t: (b, t, 0))] + [full(a) for a in operands],
        out_specs=pl.BlockSpec((1, ts, d), lambda b, t: (b, t, 0)),
        scratch_shapes=scratch,
        compiler_params=pltpu.CompilerParams(
            dimension_semantics=("arbitrary", "arbitrary"),
            vmem_limit_bytes=VMEM_LIMIT_BYTES),
        name="hybrid_layer",
    )(x, *operands)


def _pick_tile(s):
    for ts in (512, 256, 128, 64):
        if s % ts == 0:
            return ts
    raise ValueError(f"sequence length {s} must be a multiple of {CHUNK}")


def kernel(x, pre_norm_g, post_norm_g, w_in, conv_w, dn_a_log, dn_dt_bias, dn_norm_g, ssm_a_re, ssm_a_im, ssm_b_re, ssm_b_im, ssm_c_re, ssm_c_im, ssm_d, ssm_log_dt, glu_w, glu_b, w_out):
    depth = w_in.shape[0]
    ts = _pick_tile(x.shape[1])
    ssm = jax.vmap(_ssm_matrices)(ssm_a_re, ssm_a_im, ssm_b_re, ssm_b_im, ssm_c_re, ssm_c_im,
                                  ssm_d, ssm_log_dt)
    w_parts = _split_w_in(w_in)
    glu_wb = glu_w.astype(BF16)
    w_outb = w_out.astype(BF16)
    for l in range(depth):
        x = _layer(x, ts, pre_norm_g[l], post_norm_g[l], [w[l] for w in w_parts], conv_w[l], dn_a_log[l],
                   dn_dt_bias[l], dn_norm_g[l], [m[l] for m in ssm], glu_wb[l], glu_b[l], w_outb[l])
    return x
```

```python
import functools
import math

import jax
import jax.numpy as jnp
import numpy as np
from jax import lax
from jax.experimental import pallas as pl
from jax.experimental.pallas import tpu as pltpu

F32 = jnp.float32
BF16 = jnp.bfloat16

DN_HEADS = 4
DN_HEAD_DIM = 128
DN_WIDTH = DN_HEADS * DN_HEAD_DIM
CONV_WIDTH = 4
CHUNK = 64
HROWS = DN_HEADS * CHUNK
SSM_GROUP = 16
SSM_GROUPS = 32
SSM_STATE = 64
SSM_WIDTH = SSM_GROUP * SSM_GROUPS
SSM_L = 8
SSM_PAIRS = SSM_GROUPS // 2
SSM_ROW = SSM_L * SSM_GROUP
PAIR_ROW = 2 * SSM_ROW
EPS = 1e-6
NEG_LOG2E = -1.4426950408889634
LANES = 128
NORM_ROWS = 32
GDN_GROUP = 8
CONV_HDR = 8
VMEM_LIMIT_BYTES = 56 * 1024 * 1024


def _dot(a, b):
    return jnp.dot(a, b, preferred_element_type=F32)


def _dot_nt(a, b):
    return lax.dot_general(a, b, (((1,), (1,)), ((), ())), preferred_element_type=F32)


def _dot_tn(a, b):
    return lax.dot_general(a, b, (((0,), (0,)), ((), ())), preferred_element_type=F32)


def _split2(a):
    hi = a.astype(BF16)
    mid = (a - hi.astype(F32)).astype(BF16)
    return hi, mid


def _sigmoid(x):
    return 1.0 / (1.0 + jnp.exp2(x * NEG_LOG2E))


def _silu(x):
    return x * _sigmoid(x)


def _softplus(x):
    return jnp.maximum(x, 0.0) + jnp.log(1.0 + jnp.exp(-jnp.abs(x)))


def _gelu_tanh(x):
    c = math.sqrt(2.0 / math.pi)
    return 0.5 * x * (1.0 + jnp.tanh(c * (x + 0.044715 * (x * x * x))))


def _piece_transpose8(vs):
    lane = lax.broadcasted_iota(jnp.int32, vs[0].shape, 1)
    piece = lane // SSM_GROUP
    for d in (4, 2, 1):
        keep = (piece & d) == 0
        new = list(vs)
        for a in range(8):
            if a & d == 0:
                b = a + d
                new[a] = jnp.where(keep, vs[a], pltpu.roll(vs[b], d * SSM_GROUP, 1))
                new[b] = jnp.where(keep, pltpu.roll(vs[a], LANES - d * SSM_GROUP, 1), vs[b])
        vs = new
    return vs


_DONE = object()


def _weave(main, side):
    side_done = False
    for _ in main:
        if not side_done:
            side_done = next(side, _DONE) is _DONE
    for _ in side:
        pass


def _layer_kernel(x_ref, pre_g_ref, post_g_ref, wqkv_ref, wz_ref, wu_ref, wzs_ref, wsm_ref,
                  conv_ref, dnvec_ref, normg_ref, causalcat_ref, strictcat_ref,
                  eyecat_ref, mssm_ref, bpow_ref, cpow_ref, lam_ref, gluw_ref, glub_ref, wout_ref,
                  o_ref,
                  h_scr, qkv_scr, q_scr, k_scr, v_scr, gates_scr, odn_scr, state_scr,
                  oprime_scr, gq_scr, hmat_scr, elast_scr,
                  u_scr, y_scr, xp_scr, yp_scr, zdn_scr, zs_scr, ossm_scr,
                  bre_scr, bim_scr, sre_scr, sim_scr, carry_scr, *, ts):
    t = pl.program_id(1)
    nchunk = ts // CHUNK
    nblk = ts // SSM_L

    @pl.when(t == 0)
    def _():
        qkv_scr[0:CONV_HDR, :] = jnp.zeros((CONV_HDR, 3 * DN_WIDTH), F32)
        state_scr[...] = jnp.zeros(state_scr.shape, F32)
        carry_scr[...] = jnp.zeros(carry_scr.shape, F32)

    for r in range(0, ts, NORM_ROWS):
        xb = x_ref[0, r:r + NORM_ROWS, :]
        ms = jnp.mean(xb * xb, axis=-1, keepdims=True)
        h_scr[r:r + NORM_ROWS, :] = (xb * lax.rsqrt(ms + EPS) * pre_g_ref[...]).astype(BF16)
    h = h_scr[...]

    small = _dot(h, wsm_ref[...])
    qkv_scr[CONV_HDR:CONV_HDR + ts, :] = _dot(h, wqkv_ref[...])
    u_val = _dot(h, wu_ref[...])
    for vcol in range(SSM_WIDTH // LANES):
        u_scr[vcol] = u_val[:, vcol * LANES:(vcol + 1) * LANES]
    zdn_scr[...] = _dot(h, wz_ref[...])

    for cb in range(3 * DN_WIDTH // LANES):
        cols = slice(cb * LANES, (cb + 1) * LANES)
        for r in range(0, ts, CHUNK):
            xx = qkv_scr[r:r + CONV_HDR + CHUNK, cols]
            acc = conv_ref[CONV_WIDTH - 1:CONV_WIDTH, cols] * xx
            for j in range(CONV_WIDTH - 1):
                acc = acc + pltpu.roll(conv_ref[j:j + 1, cols] * xx, CONV_WIDTH - 1 - j, 0)
            c = _silu(acc[CONV_HDR:])
            if cb < DN_HEADS:
                c = c * (lax.rsqrt(jnp.sum(c * c, axis=-1, keepdims=True) + EPS) * (DN_HEAD_DIM ** -0.5))
                q_scr[r:r + CHUNK, cols] = c
            elif cb < 2 * DN_HEADS:
                c = c * lax.rsqrt(jnp.sum(c * c, axis=-1, keepdims=True) + EPS)
                k_scr[r:r + CHUNK, cb * LANES - DN_WIDTH:(cb + 1) * LANES - DN_WIDTH] = c
            else:
                v_scr[r:r + CHUNK, cb * LANES - 2 * DN_WIDTH:(cb + 1) * LANES - 2 * DN_WIDTH] = c
    qkv_scr[0:CONV_HDR, :] = qkv_scr[ts:ts + CONV_HDR, :]

    lane = lax.broadcasted_iota(jnp.int32, small.shape, 1)
    is_g = (lane >= DN_HEADS) & (lane < 2 * DN_HEADS)
    neg_a = dnvec_ref[0:1, :]
    dt_b = dnvec_ref[1:2, :]
    gates_scr[...] = jnp.where(is_g, neg_a * _softplus(small + dt_b), _sigmoid(small))
    zs_scr[...] = _dot(h, wzs_ref[...])

    for vcol in range(SSM_WIDTH // LANES):
        outs = _piece_transpose8(
            [u_scr[vcol, pl.ds(j8, nblk, stride=SSM_L), :] for j8 in range(SSM_L)])
        for g8 in range(8):
            g = vcol * 8 + g8
            xp_scr[g // 2, :, (g % 2) * SSM_ROW:(g % 2 + 1) * SSM_ROW] = outs[g8].astype(BF16)

    def ssm_state_in(pr):
        bpair = _dot(xp_scr[pr], bpow_ref[pr])
        bre_scr[:, pr * LANES:(pr + 1) * LANES] = bpair[:, :LANES]
        bim_scr[:, pr * LANES:(pr + 1) * LANES] = bpair[:, LANES:]

    def ssm_state_out(pr):
        sp = jnp.concatenate([sre_scr[:, pr * LANES:(pr + 1) * LANES],
                              sim_scr[:, pr * LANES:(pr + 1) * LANES]], axis=1).astype(BF16)
        yp_scr[pr] = _dot(sp, cpow_ref[pr]) + _dot(xp_scr[pr], mssm_ref[pr])

    lane_c = lax.broadcasted_iota(jnp.int32, (CHUNK, LANES), 1)
    low_half = lane_c < CHUNK

    def blockdiag(xc):
        xb = xc.astype(BF16)
        zero = jnp.zeros((CHUNK, LANES), BF16)
        x01, x23 = xb[:, :LANES], xb[:, LANES:]
        return jnp.concatenate([
            jnp.concatenate([jnp.where(low_half, x01, zero), zero], axis=1),
            jnp.concatenate([jnp.where(low_half, zero, x01), zero], axis=1),
            jnp.concatenate([zero, jnp.where(low_half, x23, zero)], axis=1),
            jnp.concatenate([zero, jnp.where(low_half, zero, x23)], axis=1)], axis=0)

    def headcat(m):
        return jnp.concatenate(
            [jnp.where(low_half, m[0:CHUNK, :LANES], m[CHUNK:2 * CHUNK, :LANES]),
             jnp.where(low_half, m[2 * CHUNK:3 * CHUNK, LANES:], m[3 * CHUNK:4 * CHUNK, LANES:])], axis=1)

    row_c = lax.broadcasted_iota(jnp.int32, (CHUNK, LANES), 0)

    def chunk_cumsum(x):
        for sh in (1, 2, 4, 8, 16, 32):
            x = x + jnp.where(row_c >= sh, pltpu.roll(x, sh, 0), 0.0)
        return x

    def lane_bcast(x, lane_idx):
        return jnp.broadcast_to(x[:, lane_idx:lane_idx + 1], (CHUNK, LANES))

    def chunk_setup(ci):
        rows = slice(ci * CHUNK, (ci + 1) * CHUNK)
        gch = gates_scr[rows, :]
        gsum = chunk_cumsum(gch)
        gcols = [lane_bcast(gsum, DN_HEADS + hd) for hd in range(DN_HEADS)]
        bcols = [lane_bcast(gch, hd) for hd in range(DN_HEADS)]

        def stack(ref):
            return jnp.concatenate(
                [ref[rows, hd * DN_HEAD_DIM:(hd + 1) * DN_HEAD_DIM] for hd in range(DN_HEADS)], axis=0)

        kst = stack(k_scr)
        qst = stack(q_scr)
        vst = stack(v_scr)
        gcol = jnp.concatenate(gcols, axis=0)
        bcol = jnp.concatenate(bcols, axis=0)
        kb = kst * bcol
        vb = vst * bcol
        egc = jnp.exp(gcol)
        grow = jnp.transpose(gcol)[0:1, :]
        gcol_cat = jnp.concatenate(
            [jnp.where(low_half, gcols[0], gcols[1]), jnp.where(low_half, gcols[2], gcols[3])],
            axis=1)
        causal = causalcat_ref[...]
        decay = jnp.exp(jnp.where(causal > 0.0, gcol_cat - grow, 0.0)) * causal
        kk = _dot_nt(jnp.concatenate([kb, qst], axis=0).astype(BF16), kst.astype(BF16))
        a_cat = headcat(kk[:HROWS]) * decay * strictcat_ref[...]
        intra_bd = blockdiag(headcat(kk[HROWS:]) * decay)
        rhs = jnp.concatenate([vb, kb * egc], axis=1).astype(BF16)
        qg = qst * egc
        kdecs = []
        for hd in range(DN_HEADS):
            hs = slice(hd * CHUNK, (hd + 1) * CHUNK)
            g_h = gcol[hs]
            g_last = g_h[CHUNK - 1:CHUNK, :]
            kdecs.append((kst[hs] * jnp.exp(g_last - g_h)).astype(BF16))
            elast_scr[ci, hd:hd + 1, :] = jnp.exp(g_last)
        return a_cat, rhs, qg, intra_bd, kdecs

    def delta_setup_steps():
        for g0 in range(0, nchunk, GDN_GROUP):
            group = range(g0, min(g0 + GDN_GROUP, nchunk))
            setup = []
            for ci in group:
                setup.append(chunk_setup(ci))
                yield
            t_cats = [eyecat_ref[...] - st[0] for st in setup]
            p_cats = []
            for st in setup:
                p_cats.append(_dot(st[0].astype(BF16), blockdiag(st[0])))
            yield
            for lvl in range(5):
                for k in range(len(setup)):
                    p_bd = blockdiag(p_cats[k])
                    if lvl < 4:
                        both = _dot(jnp.concatenate([t_cats[k], p_cats[k]], axis=0).astype(BF16), p_bd)
                        t_cats[k] = t_cats[k] + both[:CHUNK]
                        p_cats[k] = both[CHUNK:]
                    else:
                        t_cats[k] = t_cats[k] + _dot(t_cats[k].astype(BF16), p_bd)
                yield
            uw = []
            for k in range(len(setup)):
                sol = _dot(blockdiag(t_cats[k]), setup[k][1])
                uw.append(sol.astype(BF16))
            yield
            for k, ci in enumerate(group):
                _, _, qg, intra_bd, kdecs = setup[k]
                iuw = _dot(intra_bd, uw[k])
                oprime_scr[ci] = iuw[:, :DN_HEAD_DIM]
                qprime = (qg - iuw[:, DN_HEAD_DIM:]).astype(BF16)
                for hd in range(DN_HEADS):
                    hs = slice(hd * CHUNK, (hd + 1) * CHUNK)
                    kuw = _dot_tn(kdecs[hd], uw[k][hs])
                    hmat_scr[ci * DN_HEADS + hd] = kuw[:, :DN_HEAD_DIM]
                    gq_scr[ci * DN_HEADS + hd] = jnp.concatenate(
                        [(-kuw[:, DN_HEAD_DIM:]).astype(BF16), qprime[hs]], axis=0)
                yield

    def side_projection_steps():
        for pr in range(SSM_PAIRS):
            ssm_state_in(pr)
            yield

    _weave(delta_setup_steps(), side_projection_steps())

    lre = lam_ref[0:1, :]
    lim = lam_ref[1:2, :]
    sre = carry_scr[0:1, :]
    sim = carry_scr[1:2, :]
    for cb in range(nblk):
        sre_scr[cb:cb + 1, :] = sre
        sim_scr[cb:cb + 1, :] = sim
        bre = bre_scr[cb:cb + 1, :]
        bim = bim_scr[cb:cb + 1, :]
        sre, sim = lre * sre - lim * sim + bre, lre * sim + lim * sre + bim
    carry_scr[0:1, :] = sre
    carry_scr[1:2, :] = sim

    def delta_recurrence_steps():
        for ci in range(nchunk):
            rows = slice(ci * CHUNK, (ci + 1) * CHUNK)
            oprime = oprime_scr[ci]
            for hd in range(DN_HEADS):
                hs = slice(hd * CHUNK, (hd + 1) * CHUNK)
                idx = ci * DN_HEADS + hd
                s_old = state_scr[hd]
                gs = _dot(gq_scr[idx], s_old.astype(BF16))
                state_scr[hd] = s_old * elast_scr[ci, hd:hd + 1, :] + gs[:DN_HEAD_DIM] + hmat_scr[idx]
                odn_scr[rows, hd * DN_HEAD_DIM:(hd + 1) * DN_HEAD_DIM] = gs[DN_HEAD_DIM:] + oprime[hs]
            yield

    def ssm_output_steps():
        for pr in range(SSM_PAIRS):
            ssm_state_out(pr)
            if pr % 2 == 1:
                yield
        for vcol in range(SSM_WIDTH // LANES):
            outs = _piece_transpose8(
                [yp_scr[(vcol * 8 + g8) // 2, :, (g8 % 2) * SSM_ROW:(g8 % 2 + 1) * SSM_ROW] for g8 in range(8)])
            for i8 in range(SSM_L):
                y_scr[vcol, pl.ds(i8, nblk, stride=SSM_L), :] = outs[i8]
        y_val = jnp.concatenate([y_scr[vcol] for vcol in range(SSM_WIDTH // LANES)], axis=1)
        ge = _gelu_tanh(y_val).astype(BF16)
        gl = _dot(ge, gluw_ref[...]) + glub_ref[...]
        ossm_scr[...] = (gl[:, :SSM_WIDTH] * _sigmoid(gl[:, SSM_WIDTH:]) * _silu(zs_scr[...])).astype(BF16)
        yield

    _weave(delta_recurrence_steps(), ssm_output_steps())

    o_parts = []
    for hd in range(DN_HEADS):
        sl = slice(hd * DN_HEAD_DIM, (hd + 1) * DN_HEAD_DIM)
        oh = odn_scr[:, sl]
        on = oh * lax.rsqrt(jnp.mean(oh * oh, axis=-1, keepdims=True) + EPS) * normg_ref[...]
        o_parts.append((on * _silu(zdn_scr[:, sl])).astype(BF16))
    mix_in = jnp.concatenate(o_parts + [ossm_scr[...]], axis=1)
    mix = _dot(mix_in, wout_ref[...])
    mn = mix * lax.rsqrt(jnp.mean(mix * mix, axis=-1, keepdims=True) + EPS) * post_g_ref[...]
    o_ref[0] = x_ref[0] + mn


def _ssm_matrices(a_re, a_im, b_re, b_im, c_re, c_im, d, log_dt):
    hp = lax.Precision.HIGHEST
    dt = jnp.exp(log_dt.astype(F32))[:, None]
    a_re = a_re.astype(F32)
    a_im = a_im.astype(F32)
    mag = jnp.exp(a_re * dt)
    ang = a_im * dt
    lb_re, lb_im = mag * jnp.cos(ang), mag * jnp.sin(ang)
    den = a_re * a_re + a_im * a_im
    num_re, num_im = lb_re - 1.0, lb_im
    coef_re = (num_re * a_re + num_im * a_im) / den
    coef_im = (num_im * a_re - num_re * a_im) / den
    b_re = jnp.swapaxes(b_re.astype(F32), 1, 2)
    b_im = jnp.swapaxes(b_im.astype(F32), 1, 2)
    bb_re = coef_re[:, None, :] * b_re - coef_im[:, None, :] * b_im
    bb_im = coef_re[:, None, :] * b_im + coef_im[:, None, :] * b_re
    taus = jnp.arange(SSM_L + 1, dtype=F32)[None, :, None]
    pmag = jnp.exp((a_re * dt)[:, None, :] * taus)
    pang = (a_im * dt)[:, None, :] * taus
    lp_re, lp_im = pmag * jnp.cos(pang), pmag * jnp.sin(pang)
    c_re = c_re.astype(F32)
    c_im = c_im.astype(F32)
    cl_re = c_re[:, None] * lp_re[:, :, None, :] - c_im[:, None] * lp_im[:, :, None, :]
    cl_im = c_re[:, None] * lp_im[:, :, None, :] + c_im[:, None] * lp_re[:, :, None, :]
    kern = (jnp.einsum("gthp,gkp->tghk", cl_re[:, :SSM_L], bb_re, precision=hp)
            - jnp.einsum("gthp,gkp->tghk", cl_im[:, :SSM_L], bb_im, precision=hp))
    dmat = d.astype(F32).reshape(SSM_GROUPS, SSM_GROUP)[:, :, None] * jnp.eye(SSM_GROUP, dtype=F32)[None]
    kern = kern + jnp.asarray(_LAG0)[:, None, None, None] * dmat[None]
    mg = jnp.einsum("tji,tghk->gjkih", jnp.asarray(_LAG_SELECT), kern,
                    precision=hp).reshape(SSM_GROUPS, SSM_ROW, SSM_ROW)
    lr = lp_re[:, SSM_L - 1::-1][:, :SSM_L]
    li = lp_im[:, SSM_L - 1::-1][:, :SSM_L]
    bp_re = (lr[:, :, None, :] * bb_re[:, None] - li[:, :, None, :] * bb_im[:, None]
             ).reshape(SSM_GROUPS, SSM_ROW, SSM_STATE)
    bp_im = (lr[:, :, None, :] * bb_im[:, None] + li[:, :, None, :] * bb_re[:, None]
             ).reshape(SSM_GROUPS, SSM_ROW, SSM_STATE)
    co_re = jnp.transpose(cl_re[:, 1:], (0, 3, 1, 2)).reshape(SSM_GROUPS, SSM_STATE, SSM_ROW)
    co_im = -jnp.transpose(cl_im[:, 1:], (0, 3, 1, 2)).reshape(SSM_GROUPS, SSM_STATE, SSM_ROW)

    def pair(x):
        return x.reshape((SSM_PAIRS, 2) + x.shape[1:])

    def blocks(rows):
        return jnp.concatenate([jnp.concatenate(r, axis=2) for r in rows], axis=1)

    mp, bre, bim, cre, cim = pair(mg), pair(bp_re), pair(bp_im), pair(co_re), pair(co_im)
    zm, zb, zc = jnp.zeros_like(mp[:, 0]), jnp.zeros_like(bre[:, 0]), jnp.zeros_like(cre[:, 0])
    mssm = blocks([[mp[:, 0], zm], [zm, mp[:, 1]]])
    bpow = blocks([[bre[:, 0], zb, bim[:, 0], zb], [zb, bre[:, 1], zb, bim[:, 1]]])
    cpow = blocks([[cre[:, 0], zc], [zc, cre[:, 1]], [cim[:, 0], zc], [zc, cim[:, 1]]])
    lam = jnp.stack([lp_re[:, SSM_L].reshape(-1), lp_im[:, SSM_L].reshape(-1)], axis=0)
    return mssm.astype(BF16), bpow.astype(BF16), cpow.astype(BF16), lam


def _lag_tables():
    tau, j, i = np.meshgrid(np.arange(SSM_L), np.arange(SSM_L), np.arange(SSM_L), indexing="ij")
    return (i - j == tau).astype(np.float32), (np.arange(SSM_L) == 0).astype(np.float32)


_LAG_SELECT, _LAG0 = _lag_tables()


def _constants():
    ri = np.arange(CHUNK)[:, None]
    cj = np.arange(HROWS)[None, :] % CHUNK
    return tuple(jnp.asarray(m, dtype=F32) for m in (ri >= cj, ri > cj, ri == cj))


def _split_w_in(w_in):
    c_qkv = 3 * DN_WIDTH
    c_z = c_qkv + DN_WIDTH
    c_alpha = c_z + 2 * DN_HEADS
    c_u = c_alpha + SSM_WIDTH
    wb = w_in.astype(BF16)
    pad = [(0, 0)] * (w_in.ndim - 1) + [(0, LANES - 2 * DN_HEADS)]
    return (wb[..., :c_qkv], wb[..., c_qkv:c_z], wb[..., c_alpha:c_u], wb[..., c_u:],
            jnp.pad(wb[..., c_z:c_alpha], pad))


def _layer(x, ts, pre_g, post_g, w_parts, conv_w, a_log, dt_bias, norm_g, ssm, glu_w, glu_b, w_out):
    bsz, s, d = x.shape
    c_qkv = 3 * DN_WIDTH
    wqkv, wz, wu, wzs, wsm = w_parts
    dnvec = jnp.pad(jnp.stack([-jnp.exp(a_log.astype(F32)), dt_bias.astype(F32)]),
                    ((0, 0), (DN_HEADS, LANES - 2 * DN_HEADS)))
    mssm, bpow, cpow, lam = ssm
    consts = _constants()
    nblk = ts // SSM_L
    nstate = SSM_GROUPS * SSM_STATE

    def full(a):
        nd = a.ndim
        return pl.BlockSpec(a.shape, lambda b, t, _nd=nd: (0,) * _nd)

    operands = [
        pre_g.reshape(1, d).astype(F32), post_g.reshape(1, d).astype(F32),
        wqkv, wz, wu, wzs, wsm, conv_w.astype(F32), dnvec, norm_g.reshape(1, DN_HEAD_DIM).astype(F32),
        *consts, mssm, bpow, cpow, lam,
        glu_w, glu_b.reshape(1, -1).astype(F32), w_out]
    scratch = [
        pltpu.VMEM((ts, d), BF16),
        pltpu.VMEM((ts + CONV_HDR, c_qkv), F32),
        pltpu.VMEM((ts, DN_WIDTH), F32),
        pltpu.VMEM((ts, DN_WIDTH), F32),
        pltpu.VMEM((ts, DN_WIDTH), F32),
        pltpu.VMEM((ts, LANES), F32),
        pltpu.VMEM((ts, DN_WIDTH), F32),
        pltpu.VMEM((DN_HEADS, DN_HEAD_DIM, DN_HEAD_DIM), F32),
        pltpu.VMEM((ts // CHUNK, HROWS, DN_HEAD_DIM), F32),
        pltpu.VMEM((ts // CHUNK * DN_HEADS, DN_HEAD_DIM + CHUNK, DN_HEAD_DIM), BF16),
        pltpu.VMEM((ts // CHUNK * DN_HEADS, DN_HEAD_DIM, DN_HEAD_DIM), F32),
        pltpu.VMEM((ts // CHUNK, 8, DN_HEAD_DIM), F32),
        pltpu.VMEM((SSM_WIDTH // LANES, ts, LANES), F32),
        pltpu.VMEM((SSM_WIDTH // LANES, ts, LANES), F32),
        pltpu.VMEM((SSM_PAIRS, nblk, PAIR_ROW), BF16),
        pltpu.VMEM((SSM_PAIRS, nblk, PAIR_ROW), F32),
        pltpu.VMEM((ts, DN_WIDTH), F32),
        pltpu.VMEM((ts, SSM_WIDTH), F32),
        pltpu.VMEM((ts, SSM_WIDTH), BF16),
        pltpu.VMEM((nblk, nstate), F32),
        pltpu.VMEM((nblk, nstate), F32),
        pltpu.VMEM((nblk, nstate), F32),
        pltpu.VMEM((nblk, nstate), F32),
        pltpu.VMEM((2, nstate), F32),
    ]
    return pl.pallas_call(
        functools.partial(_layer_kernel, ts=ts),
        out_shape=jax.ShapeDtypeStruct(x.shape, x.dtype),
        grid=(bsz, s // ts),
        in_specs=[pl.BlockSpec((1, ts, d), lambda b, t: (b, t, 0))] + [full(a) for a in operands],
        out_specs=pl.BlockSpec((1, ts, d), lambda b, t: (b, t, 0)),
        scratch_shapes=scratch,
        compiler_params=pltpu.CompilerParams(
            dimension_semantics=("arbitrary", "arbitrary"),
            vmem_limit_bytes=VMEM_LIMIT_BYTES),
        name="hybrid_layer",
    )(x, *operands)


def _pick_tile(s):
    for ts in (512, 256, 128, 64):
        if s % ts == 0:
            return ts
    raise ValueError(f"sequence length {s} must be a multiple of {CHUNK}")


def kernel(x, pre_norm_g, post_norm_g, w_in, conv_w, dn_a_log, dn_dt_bias, dn_norm_g, ssm_a_re, ssm_a_im, ssm_b_re, ssm_b_im, ssm_c_re, ssm_c_im, ssm_d, ssm_log_dt, glu_w, glu_b, w_out):
    depth = w_in.shape[0]
    ts = _pick_tile(x.shape[1])
    ssm = jax.vmap(_ssm_matrices)(ssm_a_re, ssm_a_im, ssm_b_re, ssm_b_im, ssm_c_re, ssm_c_im,
                                  ssm_d, ssm_log_dt)
    w_parts = _split_w_in(w_in)
    glu_wb = glu_w.astype(BF16)
    w_outb = w_out.astype(BF16)
    for l in range(depth):
        x = _layer(x, ts, pre_norm_g[l], post_norm_g[l], [w[l] for w in w_parts], conv_w[l], dn_a_log[l],
                   dn_dt_bias[l], dn_norm_g[l], [m[l] for m in ssm], glu_wb[l], glu_b[l], w_outb[l])
    return x
```

```python
import functools
import math

import jax
import jax.numpy as jnp
import numpy as np
from jax import lax
from jax.experimental import pallas as pl
from jax.experimental.pallas import tpu as pltpu

F32 = jnp.float32
BF16 = jnp.bfloat16

DN_HEADS = 4
DN_HEAD_DIM = 128
DN_WIDTH = DN_HEADS * DN_HEAD_DIM
CONV_WIDTH = 4
CHUNK = 64
HROWS = DN_HEADS * CHUNK
SSM_GROUP = 16
SSM_GROUPS = 32
SSM_STATE = 64
SSM_WIDTH = SSM_GROUP * SSM_GROUPS
SSM_L = 8
SSM_PAIRS = SSM_GROUPS // 2
SSM_ROW = SSM_L * SSM_GROUP
PAIR_ROW = 2 * SSM_ROW
EPS = 1e-6
COL_QKV = 128
COL_U = COL_QKV + 3 * DN_WIDTH
COL_ZDN = COL_U + SSM_WIDTH
COL_ZS = COL_ZDN + DN_WIDTH
NEG_LOG2E =-1.4426950408889634
LANES = 128
NORM_ROWS = 32
GDN_GROUP = 8
CONV_HDR = 8
VMEM_LIMIT_BYTES = 56 * 1024 * 1024


def _dot(a, b):
    return jnp.dot(a, b, preferred_element_type=F32)


def _dot_nt(a, b):
    return lax.dot_general(a, b, (((1,), (1,)), ((), ())), preferred_element_type=F32)


def _dot_tn(a, b):
    return lax.dot_general(a, b, (((0,), (0,)), ((), ())), preferred_element_type=F32)


def _split2(a):
    hi = a.astype(BF16)
    mid = (a - hi.astype(F32)).astype(BF16)
    return hi, mid


def _sigmoid(x):
    return 1.0 / (1.0 + jnp.exp2(x * NEG_LOG2E))


def _silu(x):
    return x * _sigmoid(x)


def _softplus(x):
    return jnp.maximum(x, 0.0) + jnp.log(1.0 + jnp.exp(-jnp.abs(x)))


def _gelu_tanh(x):
    c = math.sqrt(2.0 / math.pi)
    return 0.5 * x * (1.0 + jnp.tanh(c * (x + 0.044715 * (x * x * x))))


def _piece_transpose8(vs):
    lane = lax.broadcasted_iota(jnp.int32, vs[0].shape, 1)
    piece = lane // SSM_GROUP
    for d in (4, 2, 1):
        keep = (piece & d) == 0
        new = list(vs)
        for a in range(8):
            if a & d == 0:
                b = a + d
                new[a] = jnp.where(keep, vs[a], pltpu.roll(vs[b], d * SSM_GROUP, 1))
                new[b] = jnp.where(keep, pltpu.roll(vs[a], LANES - d * SSM_GROUP, 1), vs[b])
        vs = new
    return vs


_DONE = object()


def _weave(main, side):
    side_done = False
    for _ in main:
        if not side_done:
            side_done = next(side, _DONE) is _DONE
    for _ in side:
        pass


def _layer_kernel(x_ref, pre_g_ref, post_g_ref, win_ref,
                  conv_ref, dnvec_ref, normg_ref, causalcat_ref, strictcat_ref,
                  eyecat_ref, mssm_ref, bpow_ref, cpow_ref, lam_ref, gluw_ref, glub_ref, wout_ref,
                  o_ref,
                  h_scr, qkv_scr, q_scr, k_scr, v_scr, gates_scr, mixin_scr, state_scr,
                  oprime_scr, gq_scr, hmat_scr, elast_scr,
                  u_scr, y_scr, xp_scr, yp_scr, zdn_scr, zs_scr,
                  bre_scr, bim_scr, sre_scr, sim_scr, carry_scr, *, ts):
    t = pl.program_id(1)
    nchunk = ts // CHUNK
    nblk = ts // SSM_L

    @pl.when(t == 0)
    def _():
        qkv_scr[0:CONV_HDR, :] = jnp.zeros((CONV_HDR, 3 * DN_WIDTH), F32)
        state_scr[...] = jnp.zeros(state_scr.shape, F32)
        carry_scr[...] = jnp.zeros(carry_scr.shape, F32)

    for r in range(0, ts, NORM_ROWS):
        xb = x_ref[0, r:r + NORM_ROWS, :]
        ms = jnp.mean(xb * xb, axis=-1, keepdims=True)
        h_scr[r:r + NORM_ROWS, :] = (xb * lax.rsqrt(ms + EPS) * pre_g_ref[...]).astype(BF16)
    h = h_scr[...]

    p = _dot(h, win_ref[...])
    small = p[:, :LANES]
    qkv_scr[CONV_HDR:CONV_HDR + ts, :] = p[:, COL_QKV:COL_U]
    for vcol in range(SSM_WIDTH // LANES):
        u_scr[vcol] = p[:, COL_U + vcol * LANES:COL_U + (vcol + 1) * LANES]
    zdn_scr[...] = p[:, COL_ZDN:COL_ZS]
    zs_scr[...] = p[:, COL_ZS:]

    for cb in range(3 * DN_WIDTH // LANES):
        cols = slice(cb * LANES, (cb + 1) * LANES)
        for r in range(0, ts, CHUNK):
            xx = qkv_scr[r:r + CONV_HDR + CHUNK, cols]
            acc = conv_ref[CONV_WIDTH - 1:CONV_WIDTH, cols] * xx
            for j in range(CONV_WIDTH - 1):
                acc = acc + pltpu.roll(conv_ref[j:j + 1, cols] * xx, CONV_WIDTH - 1 - j, 0)
            c = _silu(acc[CONV_HDR:])
            if cb < DN_HEADS:
                c = c * (lax.rsqrt(jnp.sum(c * c, axis=-1, keepdims=True) + EPS) * (DN_HEAD_DIM ** -0.5))
                q_scr[r:r + CHUNK, cols] = c
            elif cb < 2 * DN_HEADS:
                c = c * lax.rsqrt(jnp.sum(c * c, axis=-1, keepdims=True) + EPS)
                k_scr[r:r + CHUNK, cb * LANES - DN_WIDTH:(cb + 1) * LANES - DN_WIDTH] = c
            else:
                v_scr[r:r + CHUNK, cb * LANES - 2 * DN_WIDTH:(cb + 1) * LANES - 2 * DN_WIDTH] = c
    qkv_scr[0:CONV_HDR, :] = qkv_scr[ts:ts + CONV_HDR, :]

    lane = lax.broadcasted_iota(jnp.int32, small.shape, 1)
    is_g = (lane >= DN_HEADS) & (lane < 2 * DN_HEADS)
    neg_a = dnvec_ref[0:1, :]
    dt_b = dnvec_ref[1:2, :]
    gates_scr[...] = jnp.where(is_g, neg_a * _softplus(small + dt_b), _sigmoid(small))

    for vcol in range(SSM_WIDTH // LANES):
        outs = _piece_transpose8(
            [u_scr[vcol, pl.ds(j8, nblk, stride=SSM_L), :] for j8 in range(SSM_L)])
        for g8 in range(8):
            g = vcol * 8 + g8
            xp_scr[g // 2, :, (g % 2) * SSM_ROW:(g % 2 + 1) * SSM_ROW] = outs[g8].astype(BF16)

    def ssm_state_in(pr):
        bpair = _dot(xp_scr[pr], bpow_ref[pr])
        bre_scr[:, pr * LANES:(pr + 1) * LANES] = bpair[:, :LANES]
        bim_scr[:, pr * LANES:(pr + 1) * LANES] = bpair[:, LANES:]

    def ssm_state_out(pr):
        sp = jnp.concatenate([sre_scr[:, pr * LANES:(pr + 1) * LANES],
                              sim_scr[:, pr * LANES:(pr + 1) * LANES]], axis=1).astype(BF16)
        yp_scr[pr] = _dot(sp, cpow_ref[pr]) + _dot(xp_scr[pr], mssm_ref[pr])

    lane_c = lax.broadcasted_iota(jnp.int32, (CHUNK, LANES), 1)
    low_half = lane_c < CHUNK

    def blockdiag(xc):
        xb = xc.astype(BF16)
        zero = jnp.zeros((CHUNK, LANES), BF16)
        x01, x23 = xb[:, :LANES], xb[:, LANES:]
        return jnp.concatenate([
            jnp.concatenate([jnp.where(low_half, x01, zero), zero], axis=1),
            jnp.concatenate([jnp.where(low_half, zero, x01), zero], axis=1),
            jnp.concatenate([zero, jnp.where(low_half, x23, zero)], axis=1),
            jnp.concatenate([zero, jnp.where(low_half, zero, x23)], axis=1)], axis=0)

    def headcat(m):
        return jnp.concatenate(
            [jnp.where(low_half, m[0:CHUNK, :LANES], m[CHUNK:2 * CHUNK, :LANES]),
             jnp.where(low_half, m[2 * CHUNK:3 * CHUNK, LANES:], m[3 * CHUNK:4 * CHUNK, LANES:])], axis=1)

    row_c = lax.broadcasted_iota(jnp.int32, (CHUNK, LANES), 0)

    def chunk_cumsum(x):
        for sh in (1, 2, 4, 8, 16, 32):
            x = x + jnp.where(row_c >= sh, pltpu.roll(x, sh, 0), 0.0)
        return x

    def lane_bcast(x, lane_idx):
        return jnp.broadcast_to(x[:, lane_idx:lane_idx + 1], (CHUNK, LANES))

    def chunk_setup(ci):
        rows = slice(ci * CHUNK, (ci + 1) * CHUNK)
        gch = gates_scr[rows, :]
        gsum = chunk_cumsum(gch)
        gcols = [lane_bcast(gsum, DN_HEADS + hd) for hd in range(DN_HEADS)]
        bcols = [lane_bcast(gch, hd) for hd in range(DN_HEADS)]

        def stack(ref):
            return jnp.concatenate(
                [ref[rows, hd * DN_HEAD_DIM:(hd + 1) * DN_HEAD_DIM] for hd in range(DN_HEADS)], axis=0)

        kst = stack(k_scr)
        qst = stack(q_scr)
        vst = stack(v_scr)
        gcol = jnp.concatenate(gcols, axis=0)
        bcol = jnp.concatenate(bcols, axis=0)
        kb = kst * bcol
        vb = vst * bcol
        egc = jnp.exp(gcol)
        grow = jnp.transpose(gcol)[0:1, :]
        gcol_cat = jnp.concatenate(
            [jnp.where(low_half, gcols[0], gcols[1]), jnp.where(low_half, gcols[2], gcols[3])],
            axis=1)
        causal = causalcat_ref[...]
        decay = jnp.exp(jnp.where(causal > 0.0, gcol_cat - grow, 0.0)) * causal
        kk = _dot_nt(jnp.concatenate([kb, qst], axis=0).astype(BF16), kst.astype(BF16))
        a_cat = headcat(kk[:HROWS]) * decay * strictcat_ref[...]
        intra_bd = blockdiag(headcat(kk[HROWS:]) * decay)
        rhs = jnp.concatenate([vb, kb * egc], axis=1).astype(BF16)
        qg = qst * egc
        kdecs = []
        for hd in range(DN_HEADS):
            hs = slice(hd * CHUNK, (hd + 1) * CHUNK)
            g_h = gcol[hs]
            g_last = g_h[CHUNK - 1:CHUNK, :]
            kdecs.append((kst[hs] * jnp.exp(g_last - g_h)).astype(BF16))
            elast_scr[ci, hd:hd + 1, :] = jnp.exp(g_last)
        return a_cat, rhs, qg, intra_bd, kdecs

    def delta_setup_steps():
        for g0 in range(0, nchunk, GDN_GROUP):
            group = range(g0, min(g0 + GDN_GROUP, nchunk))
            setup = []
            for ci in group:
                setup.append(chunk_setup(ci))
                yield
            t_cats = [eyecat_ref[...] - st[0] for st in setup]
            p_cats = []
            for st in setup:
                p_cats.append(_dot(st[0].astype(BF16), blockdiag(st[0])))
            yield
            for lvl in range(5):
                for k in range(len(setup)):
                    p_bd = blockdiag(p_cats[k])
                    if lvl < 4:
                        both = _dot(jnp.concatenate([t_cats[k], p_cats[k]], axis=0).astype(BF16), p_bd)
                        t_cats[k] = t_cats[k] + both[:CHUNK]
                        p_cats[k] = both[CHUNK:]
                    else:
                        t_cats[k] = t_cats[k] + _dot(t_cats[k].astype(BF16), p_bd)
                yield
            uw = []
            for k in range(len(setup)):
                sol = _dot(blockdiag(t_cats[k]), setup[k][1])
                uw.append(sol.astype(BF16))
            yield
            for k, ci in enumerate(group):
                _, _, qg, intra_bd, kdecs = setup[k]
                iuw = _dot(intra_bd, uw[k])
                oprime_scr[ci] = iuw[:, :DN_HEAD_DIM]
                qprime = (qg - iuw[:, DN_HEAD_DIM:]).astype(BF16)
                for hd in range(DN_HEADS):
                    hs = slice(hd * CHUNK, (hd + 1) * CHUNK)
                    kuw = _dot_tn(kdecs[hd], uw[k][hs])
                    hmat_scr[ci * DN_HEADS + hd] = kuw[:, :DN_HEAD_DIM]
                    gq_scr[ci * DN_HEADS + hd] = jnp.concatenate(
                        [(-kuw[:, DN_HEAD_DIM:]).astype(BF16), qprime[hs]], axis=0)
                yield

    def side_projection_steps():
        for pr in range(SSM_PAIRS):
            ssm_state_in(pr)
            yield

    _weave(delta_setup_steps(), side_projection_steps())

    lre = lam_ref[0:1, :]
    lim = lam_ref[1:2, :]
    sre = carry_scr[0:1, :]
    sim = carry_scr[1:2, :]
    for cb in range(nblk):
        sre_scr[cb:cb + 1, :] = sre
        sim_scr[cb:cb + 1, :] = sim
        bre = bre_scr[cb:cb + 1, :]
        bim = bim_scr[cb:cb + 1, :]
        sre, sim = lre * sre - lim * sim + bre, lre * sim + lim * sre + bim
    carry_scr[0:1, :] = sre
    carry_scr[1:2, :] = sim

    def delta_recurrence_steps():
        for ci in range(nchunk):
            rows = slice(ci * CHUNK, (ci + 1) * CHUNK)
            oprime = oprime_scr[ci]
            for hd in range(DN_HEADS):
                hs = slice(hd * CHUNK, (hd + 1) * CHUNK)
                idx = ci * DN_HEADS + hd
                s_old = state_scr[hd]
                gs = _dot(gq_scr[idx], s_old.astype(BF16))
                state_scr[hd] = s_old * elast_scr[ci, hd:hd + 1, :] + gs[:DN_HEAD_DIM] + hmat_scr[idx]
                sl = slice(hd * DN_HEAD_DIM, (hd + 1) * DN_HEAD_DIM)
                oh = gs[DN_HEAD_DIM:] + oprime[hs]
                on = oh * lax.rsqrt(jnp.mean(oh * oh, axis=-1, keepdims=True) + EPS) * normg_ref[...]
                mixin_scr[rows, sl] = (on * _silu(zdn_scr[rows, sl])).astype(BF16)
            yield

    def ssm_output_steps():
        for pr in range(SSM_PAIRS):
            ssm_state_out(pr)
            if pr % 2 == 1:
                yield
        for vcol in range(SSM_WIDTH // LANES):
            outs = _piece_transpose8(
                [yp_scr[(vcol * 8 + g8) // 2, :, (g8 % 2) * SSM_ROW:(g8 % 2 + 1) * SSM_ROW] for g8 in range(8)])
            for i8 in range(SSM_L):
                y_scr[vcol, pl.ds(i8, nblk, stride=SSM_L), :] = outs[i8]
        y_val = jnp.concatenate([y_scr[vcol] for vcol in range(SSM_WIDTH // LANES)], axis=1)
        ge = _gelu_tanh(y_val).astype(BF16)
        gl = _dot(ge, gluw_ref[...]) + glub_ref[...]
        mixin_scr[:, DN_WIDTH:] = (gl[:, :SSM_WIDTH] * _sigmoid(gl[:, SSM_WIDTH:])
                                   * _silu(zs_scr[...])).astype(BF16)
        yield

    _weave(delta_recurrence_steps(), ssm_output_steps())

    mix = _dot(mixin_scr[...], wout_ref[...])
    mn = mix * lax.rsqrt(jnp.mean(mix * mix, axis=-1, keepdims=True) + EPS) * post_g_ref[...]
    o_ref[0] = x_ref[0] + mn


def _ssm_matrices(a_re, a_im, b_re, b_im, c_re, c_im, d, log_dt):
    hp = lax.Precision.HIGHEST
    dt = jnp.exp(log_dt.astype(F32))[:, None]
    a_re = a_re.astype(F32)
    a_im = a_im.astype(F32)
    mag = jnp.exp(a_re * dt)
    ang = a_im * dt
    lb_re, lb_im = mag * jnp.cos(ang), mag * jnp.sin(ang)
    den = a_re * a_re + a_im * a_im
    num_re, num_im = lb_re - 1.0, lb_im
    coef_re = (num_re * a_re + num_im * a_im) / den
    coef_im = (num_im * a_re - num_re * a_im) / den
    b_re = jnp.swapaxes(b_re.astype(F32), 1, 2)
    b_im = jnp.swapaxes(b_im.astype(F32), 1, 2)
    bb_re = coef_re[:, None, :] * b_re - coef_im[:, None, :] * b_im
    bb_im = coef_re[:, None, :] * b_im + coef_im[:, None, :] * b_re
    taus = jnp.arange(SSM_L + 1, dtype=F32)[None, :, None]
    pmag = jnp.exp((a_re * dt)[:, None, :] * taus)
    pang = (a_im * dt)[:, None, :] * taus
    lp_re, lp_im = pmag * jnp.cos(pang), pmag * jnp.sin(pang)
    c_re = c_re.astype(F32)
    c_im = c_im.astype(F32)
    cl_re = c_re[:, None] * lp_re[:, :, None, :] - c_im[:, None] * lp_im[:, :, None, :]
    cl_im = c_re[:, None] * lp_im[:, :, None, :] + c_im[:, None] * lp_re[:, :, None, :]
    kern = (jnp.einsum("gthp,gkp->tghk", cl_re[:, :SSM_L], bb_re, precision=hp)
            - jnp.einsum("gthp,gkp->tghk", cl_im[:, :SSM_L], bb_im, precision=hp))
    dmat = d.astype(F32).reshape(SSM_GROUPS, SSM_GROUP)[:, :, None] * jnp.eye(SSM_GROUP, dtype=F32)[None]
    kern = kern + jnp.asarray(_LAG0)[:, None, None, None] * dmat[None]
    mg = jnp.einsum("tji,tghk->gjkih", jnp.asarray(_LAG_SELECT), kern,
                    precision=hp).reshape(SSM_GROUPS, SSM_ROW, SSM_ROW)
    lr = lp_re[:, SSM_L - 1::-1][:, :SSM_L]
    li = lp_im[:, SSM_L - 1::-1][:, :SSM_L]
    bp_re = (lr[:, :, None, :] * bb_re[:, None] - li[:, :, None, :] * bb_im[:, None]
             ).reshape(SSM_GROUPS, SSM_ROW, SSM_STATE)
    bp_im = (lr[:, :, None, :] * bb_im[:, None] + li[:, :, None, :] * bb_re[:, None]
             ).reshape(SSM_GROUPS, SSM_ROW, SSM_STATE)
    co_re = jnp.transpose(cl_re[:, 1:], (0, 3, 1, 2)).reshape(SSM_GROUPS, SSM_STATE, SSM_ROW)
    co_im = -jnp.transpose(cl_im[:, 1:], (0, 3, 1, 2)).reshape(SSM_GROUPS, SSM_STATE, SSM_ROW)

    def pair(x):
        return x.reshape((SSM_PAIRS, 2) + x.shape[1:])

    def blocks(rows):
        return jnp.concatenate([jnp.concatenate(r, axis=2) for r in rows], axis=1)

    mp, bre, bim, cre, cim = pair(mg), pair(bp_re), pair(bp_im), pair(co_re), pair(co_im)
    zm, zb, zc = jnp.zeros_like(mp[:, 0]), jnp.zeros_like(bre[:, 0]), jnp.zeros_like(cre[:, 0])
    mssm = blocks([[mp[:, 0], zm], [zm, mp[:, 1]]])
    bpow = blocks([[bre[:, 0], zb, bim[:, 0], zb], [zb, bre[:, 1], zb, bim[:, 1]]])
    cpow = blocks([[cre[:, 0], zc], [zc, cre[:, 1]], [cim[:, 0], zc], [zc, cim[:, 1]]])
    lam = jnp.stack([lp_re[:, SSM_L].reshape(-1), lp_im[:, SSM_L].reshape(-1)], axis=0)
    return mssm.astype(BF16), bpow.astype(BF16), cpow.astype(BF16), lam


def _lag_tables():
    tau, j, i = np.meshgrid(np.arange(SSM_L), np.arange(SSM_L), np.arange(SSM_L), indexing="ij")
    return (i - j == tau).astype(np.float32), (np.arange(SSM_L) == 0).astype(np.float32)


_LAG_SELECT, _LAG0 = _lag_tables()


def _constants():
    ri = np.arange(CHUNK)[:, None]
    cj = np.arange(HROWS)[None, :] % CHUNK
    return tuple(jnp.asarray(m, dtype=F32) for m in (ri >= cj, ri > cj, ri == cj))


def _reorder_w_in(w_in):
    c_qkv = 3 * DN_WIDTH
    c_z = c_qkv + DN_WIDTH
    c_alpha = c_z + 2 * DN_HEADS
    c_u = c_alpha + SSM_WIDTH
    wb = w_in.astype(BF16)
    pad = [(0, 0)] * (w_in.ndim - 1) + [(0, LANES - 2 * DN_HEADS)]
    return jnp.concatenate([jnp.pad(wb[..., c_z:c_alpha], pad), wb[..., :c_qkv], wb[..., c_alpha:c_u],
                            wb[..., c_qkv:c_z], wb[..., c_u:]], axis=-1)


def _layer(x, ts, pre_g, post_g, w_in, conv_w, a_log, dt_bias, norm_g, ssm, glu_w, glu_b, w_out):
    bsz, s, d = x.shape
    c_qkv = 3 * DN_WIDTH
    dnvec = jnp.pad(jnp.stack([-jnp.exp(a_log.astype(F32)), dt_bias.astype(F32)]),
                    ((0, 0), (DN_HEADS, LANES - 2 * DN_HEADS)))
    mssm, bpow, cpow, lam = ssm
    consts = _constants()
    nblk = ts // SSM_L
    nstate = SSM_GROUPS * SSM_STATE

    def full(a):
        nd = a.ndim
        return pl.BlockSpec(a.shape, lambda b, t, _nd=nd: (0,) * _nd)

    operands = [
        pre_g.reshape(1, d).astype(F32), post_g.reshape(1, d).astype(F32),
        w_in, conv_w.astype(F32), dnvec, norm_g.reshape(1, DN_HEAD_DIM).astype(F32),
        *consts, mssm, bpow, cpow, lam,
        glu_w, glu_b.reshape(1, -1).astype(F32), w_out]
    scratch = [
        pltpu.VMEM((ts, d), BF16),
        pltpu.VMEM((ts + CONV_HDR, c_qkv), F32),
        pltpu.VMEM((ts, DN_WIDTH), F32),
        pltpu.VMEM((ts, DN_WIDTH), F32),
        pltpu.VMEM((ts, DN_WIDTH), F32),
        pltpu.VMEM((ts, LANES), F32),
        pltpu.VMEM((ts, d), BF16),
        pltpu.VMEM((DN_HEADS, DN_HEAD_DIM, DN_HEAD_DIM), F32),
        pltpu.VMEM((ts // CHUNK, HROWS, DN_HEAD_DIM), F32),
        pltpu.VMEM((ts // CHUNK * DN_HEADS, DN_HEAD_DIM + CHUNK, DN_HEAD_DIM), BF16),
        pltpu.VMEM((ts // CHUNK * DN_HEADS, DN_HEAD_DIM, DN_HEAD_DIM), F32),
        pltpu.VMEM((ts // CHUNK, 8, DN_HEAD_DIM), F32),
        pltpu.VMEM((SSM_WIDTH // LANES, ts, LANES), F32),
        pltpu.VMEM((SSM_WIDTH // LANES, ts, LANES), F32),
        pltpu.VMEM((SSM_PAIRS, nblk, PAIR_ROW), BF16),
        pltpu.VMEM((SSM_PAIRS, nblk, PAIR_ROW), F32),
        pltpu.VMEM((ts, DN_WIDTH), F32),
        pltpu.VMEM((ts, SSM_WIDTH), F32),
        pltpu.VMEM((nblk, nstate), F32),
        pltpu.VMEM((nblk, nstate), F32),
        pltpu.VMEM((nblk, nstate), F32),
        pltpu.VMEM((nblk, nstate), F32),
        pltpu.VMEM((2, nstate), F32),
    ]
    return pl.pallas_call(
        functools.partial(_layer_kernel, ts=ts),
        out_shape=jax.ShapeDtypeStruct(x.shape, x.dtype),
        grid=(bsz, s // ts),
        in_specs=[pl.BlockSpec((1, ts, d), lambda b, t: (b, t, 0))] + [full(a) for a in operands],
        out_specs=pl.BlockSpec((1, ts, d), lambda b, t: (b, t, 0)),
        scratch_shapes=scratch,
        compiler_params=pltpu.CompilerParams(
            dimension_semantics=("arbitrary", "arbitrary"),
            vmem_limit_bytes=VMEM_LIMIT_BYTES),
        name="hybrid_layer",
    )(x, *operands)


def _pick_tile(s):
    for ts in (512, 256, 128, 64):
        if s % ts == 0:
            return ts
    raise ValueError(f"sequence length {s} must be a multiple of {CHUNK}")


def kernel(x, pre_norm_g, post_norm_g, w_in, conv_w, dn_a_log, dn_dt_bias, dn_norm_g, ssm_a_re, ssm_a_im, ssm_b_re, ssm_b_im, ssm_c_re, ssm_c_im, ssm_d, ssm_log_dt, glu_w, glu_b, w_out):
    depth = w_in.shape[0]
    ts = _pick_tile(x.shape[1])
    ssm = jax.vmap(_ssm_matrices)(ssm_a_re, ssm_a_im, ssm_b_re, ssm_b_im, ssm_c_re, ssm_c_im,
                                  ssm_d, ssm_log_dt)
    w_inb = _reorder_w_in(w_in)
    glu_wb = glu_w.astype(BF16)
    w_outb = w_out.astype(BF16)
    for l in range(depth):
        x = _layer(x, ts, pre_norm_g[l], post_norm_g[l], w_inb[l], conv_w[l], dn_a_log[l],
                   dn_dt_bias[l], dn_norm_g[l], [m[l] for m in ssm], glu_wb[l], glu_b[l], w_outb[l])
    return x
```

```python
import functools
import math

import jax
import jax.numpy as jnp
import numpy as np
from jax import lax
from jax.experimental import pallas as pl
from jax.experimental.pallas import tpu as pltpu

F32 = jnp.float32
BF16 = jnp.bfloat16

DN_HEADS = 4
DN_HEAD_DIM = 128
DN_WIDTH = DN_HEADS * DN_HEAD_DIM
CONV_WIDTH = 4
CHUNK = 64
HROWS = DN_HEADS * CHUNK
SSM_GROUP = 16
SSM_GROUPS = 32
SSM_STATE = 64
SSM_WIDTH = SSM_GROUP * SSM_GROUPS
SSM_L = 8
SSM_PAIRS = SSM_GROUPS // 2
SSM_ROW = SSM_L * SSM_GROUP
PAIR_ROW = 2 * SSM_ROW
EPS = 1e-6
NEG_LOG2E = -1.4426950408889634
LANES = 128
NORM_ROWS = 32
GDN_GROUP = 8
CONV_HDR = 8
VMEM_LIMIT_BYTES = 56 * 1024 * 1024


def _dot(a, b):
    return jnp.dot(a, b, preferred_element_type=F32)


def _dot_nt(a, b):
    return lax.dot_general(a, b, (((1,), (1,)), ((), ())), preferred_element_type=F32)


def _dot_tn(a, b):
    return lax.dot_general(a, b, (((0,), (0,)), ((), ())), preferred_element_type=F32)


def _split2(a):
    hi = a.astype(BF16)
    mid = (a - hi.astype(F32)).astype(BF16)
    return hi, mid


def _sigmoid(x):
    return 1.0 / (1.0 + jnp.exp2(x * NEG_LOG2E))


def _silu(x):
    return x * _sigmoid(x)


def _softplus(x):
    return jnp.maximum(x, 0.0) + jnp.log(1.0 + jnp.exp(-jnp.abs(x)))


def _gelu_tanh(x):
    c = math.sqrt(2.0 / math.pi)
    return 0.5 * x * (1.0 + jnp.tanh(c * (x + 0.044715 * (x * x * x))))


def _piece_transpose8(vs):
    lane = lax.broadcasted_iota(jnp.int32, vs[0].shape, 1)
    piece = lane // SSM_GROUP
    for d in (4, 2, 1):
        keep = (piece & d) == 0
        new = list(vs)
        for a in range(8):
            if a & d == 0:
                b = a + d
                new[a] = jnp.where(keep, vs[a], pltpu.roll(vs[b], d * SSM_GROUP, 1))
                new[b] = jnp.where(keep, pltpu.roll(vs[a], LANES - d * SSM_GROUP, 1), vs[b])
        vs = new
    return vs


_DONE = object()


def _weave(main, side):
    side_done = False
    for _ in main:
        if not side_done:
            side_done = next(side, _DONE) is _DONE
    for _ in side:
        pass


def _layer_kernel(x_ref, pre_g_ref, post_g_ref, wqkv_ref, wz_ref, wu_ref, wzs_ref, wsm_ref,
                  conv_ref, dnvec_ref, normg_ref, causalcat_ref, strictcat_ref,
                  eyecat_ref, mssm_ref, bpow_ref, cpow_ref, lam_ref, gluw_ref, glub_ref, wout_ref,
                  o_ref,
                  h_scr, qkv_scr, q_scr, k_scr, v_scr, gates_scr, odn_scr, state_scr,
                  oprime_scr, gq_scr, hmat_scr, elast_scr,
                  u_scr, y_scr, xp_scr, yp_scr, zdn_scr, zs_scr, ossm_scr,
                  bre_scr, bim_scr, sre_scr, sim_scr, carry_scr, *, ts):
    t = pl.program_id(1)
    nchunk = ts // CHUNK
    nblk = ts // SSM_L

    @pl.when(t == 0)
    def _():
        qkv_scr[0:CONV_HDR, :] = jnp.zeros((CONV_HDR, 3 * DN_WIDTH), F32)
        state_scr[...] = jnp.zeros(state_scr.shape, F32)
        carry_scr[...] = jnp.zeros(carry_scr.shape, F32)

    for r in range(0, ts, NORM_ROWS):
        xb = x_ref[0, r:r + NORM_ROWS, :]
        ms = jnp.mean(xb * xb, axis=-1, keepdims=True)
        h_scr[r:r + NORM_ROWS, :] = (xb * lax.rsqrt(ms + EPS) * pre_g_ref[...]).astype(BF16)
    h = h_scr[...]

    small = _dot(h, wsm_ref[...])
    qkv_scr[CONV_HDR:CONV_HDR + ts, :] = _dot(h, wqkv_ref[...])
    u_val = _dot(h, wu_ref[...])
    for vcol in range(SSM_WIDTH // LANES):
        u_scr[vcol] = u_val[:, vcol * LANES:(vcol + 1) * LANES]
    zdn_scr[...] = _dot(h, wz_ref[...])

    for cb in range(3 * DN_WIDTH // LANES):
        cols = slice(cb * LANES, (cb + 1) * LANES)
        for r in range(0, ts, CHUNK):
            xx = qkv_scr[r:r + CONV_HDR + CHUNK, cols]
            acc = conv_ref[CONV_WIDTH - 1:CONV_WIDTH, cols] * xx
            for j in range(CONV_WIDTH - 1):
                acc = acc + pltpu.roll(conv_ref[j:j + 1, cols] * xx, CONV_WIDTH - 1 - j, 0)
            c = _silu(acc[CONV_HDR:])
            if cb < DN_HEADS:
                c = c * (lax.rsqrt(jnp.sum(c * c, axis=-1, keepdims=True) + EPS) * (DN_HEAD_DIM ** -0.5))
                q_scr[r:r + CHUNK, cols] = c
            elif cb < 2 * DN_HEADS:
                c = c * lax.rsqrt(jnp.sum(c * c, axis=-1, keepdims=True) + EPS)
                k_scr[r:r + CHUNK, cb * LANES - DN_WIDTH:(cb + 1) * LANES - DN_WIDTH] = c
            else:
                v_scr[r:r + CHUNK, cb * LANES - 2 * DN_WIDTH:(cb + 1) * LANES - 2 * DN_WIDTH] = c
    qkv_scr[0:CONV_HDR, :] = qkv_scr[ts:ts + CONV_HDR, :]

    lane = lax.broadcasted_iota(jnp.int32, small.shape, 1)
    is_g = (lane >= DN_HEADS) & (lane < 2 * DN_HEADS)
    neg_a = dnvec_ref[0:1, :]
    dt_b = dnvec_ref[1:2, :]
    gates_scr[...] = jnp.where(is_g, neg_a * _softplus(small + dt_b), _sigmoid(small))
    zs_scr[...] = _dot(h, wzs_ref[...])

    for vcol in range(SSM_WIDTH // LANES):
        outs = _piece_transpose8(
            [u_scr[vcol, pl.ds(j8, nblk, stride=SSM_L), :] for j8 in range(SSM_L)])
        for g8 in range(8):
            g = vcol * 8 + g8
            xp_scr[g // 2, :, (g % 2) * SSM_ROW:(g % 2 + 1) * SSM_ROW] = outs[g8].astype(BF16)

    def ssm_state_in(pr):
        bpair = _dot(xp_scr[pr], bpow_ref[pr])
        bre_scr[:, pr * LANES:(pr + 1) * LANES] = bpair[:, :LANES]
        bim_scr[:, pr * LANES:(pr + 1) * LANES] = bpair[:, LANES:]

    def ssm_state_out(pr):
        sp = jnp.concatenate([sre_scr[:, pr * LANES:(pr + 1) * LANES],
                              sim_scr[:, pr * LANES:(pr + 1) * LANES]], axis=1).astype(BF16)
        yp_scr[pr] = _dot(sp, cpow_ref[pr]) + _dot(xp_scr[pr], mssm_ref[pr])

    lane_c = lax.broadcasted_iota(jnp.int32, (CHUNK, LANES), 1)
    low_half = lane_c < CHUNK

    def blockdiag(xc):
        xb = xc.astype(BF16)
        zero = jnp.zeros((CHUNK, LANES), BF16)
        x01, x23 = xb[:, :LANES], xb[:, LANES:]
        return jnp.concatenate([
            jnp.concatenate([jnp.where(low_half, x01, zero), zero], axis=1),
            jnp.concatenate([jnp.where(low_half, zero, x01), zero], axis=1),
            jnp.concatenate([zero, jnp.where(low_half, x23, zero)], axis=1),
            jnp.concatenate([zero, jnp.where(low_half, zero, x23)], axis=1)], axis=0)

    def headcat(m):
        return jnp.concatenate(
            [jnp.where(low_half, m[0:CHUNK, :LANES], m[CHUNK:2 * CHUNK, :LANES]),
             jnp.where(low_half, m[2 * CHUNK:3 * CHUNK, LANES:], m[3 * CHUNK:4 * CHUNK, LANES:])], axis=1)

    row_c = lax.broadcasted_iota(jnp.int32, (CHUNK, LANES), 0)

    def chunk_cumsum(x):
        for sh in (1, 2, 4, 8, 16, 32):
            x = x + jnp.where(row_c >= sh, pltpu.roll(x, sh, 0), 0.0)
        return x

    def lane_bcast(x, lane_idx):
        return jnp.broadcast_to(x[:, lane_idx:lane_idx + 1], (CHUNK, LANES))

    def chunk_setup(ci):
        rows = slice(ci * CHUNK, (ci + 1) * CHUNK)
        gch = gates_scr[rows, :]
        gsum = chunk_cumsum(gch)
        gcols = [lane_bcast(gsum, DN_HEADS + hd) for hd in range(DN_HEADS)]
        bcols = [lane_bcast(gch, hd) for hd in range(DN_HEADS)]

        def stack(ref):
            return jnp.concatenate(
                [ref[rows, hd * DN_HEAD_DIM:(hd + 1) * DN_HEAD_DIM] for hd in range(DN_HEADS)], axis=0)

        kst = stack(k_scr)
        qst = stack(q_scr)
        vst = stack(v_scr)
        gcol = jnp.concatenate(gcols, axis=0)
        bcol = jnp.concatenate(bcols, axis=0)
        egc = jnp.exp(gcol)
        kst_b = kst.astype(BF16)
        bcol_b = bcol.astype(BF16)
        kb = kst_b * bcol_b
        vb = vst.astype(BF16) * bcol_b
        grow = jnp.transpose(gcol)[0:1, :]
        gcol_cat = jnp.concatenate(
            [jnp.where(low_half, gcols[0], gcols[1]), jnp.where(low_half, gcols[2], gcols[3])],
            axis=1)
        causal = causalcat_ref[...]
        decay = jnp.exp(jnp.where(causal > 0.0, gcol_cat - grow, 0.0)) * causal
        kk = _dot_nt(jnp.concatenate([kb, qst.astype(BF16)], axis=0), kst_b)
        a_cat = headcat(kk[:HROWS]) * decay * strictcat_ref[...]
        intra_bd = blockdiag(headcat(kk[HROWS:]) * decay)
        rhs = jnp.concatenate([vb, kb * egc.astype(BF16)], axis=1)
        qg = qst * egc
        kdecs = []
        for hd in range(DN_HEADS):
            hs = slice(hd * CHUNK, (hd + 1) * CHUNK)
            g_h = gcol[hs]
            g_last = g_h[CHUNK - 1:CHUNK, :]
            kdecs.append(kst_b[hs] * jnp.exp(g_last - g_h).astype(BF16))
            elast_scr[ci, hd:hd + 1, :] = jnp.exp(g_last)
        return a_cat, rhs, qg, intra_bd, kdecs

    def delta_setup_steps():
        for g0 in range(0, nchunk, GDN_GROUP):
            group = range(g0, min(g0 + GDN_GROUP, nchunk))
            setup = []
            for ci in group:
                setup.append(chunk_setup(ci))
                yield
            t_cats = [eyecat_ref[...] - st[0] for st in setup]
            p_cats = []
            for st in setup:
                p_cats.append(_dot(st[0].astype(BF16), blockdiag(st[0])))
            yield
            for lvl in range(5):
                for k in range(len(setup)):
                    p_bd = blockdiag(p_cats[k])
                    if lvl < 4:
                        both = _dot(jnp.concatenate([t_cats[k], p_cats[k]], axis=0).astype(BF16), p_bd)
                        t_cats[k] = t_cats[k] + both[:CHUNK]
                        p_cats[k] = both[CHUNK:]
                    else:
                        t_cats[k] = t_cats[k] + _dot(t_cats[k].astype(BF16), p_bd)
                yield
            uw = []
            for k in range(len(setup)):
                sol = _dot(blockdiag(t_cats[k]), setup[k][1])
                uw.append(sol.astype(BF16))
            yield
            for k, ci in enumerate(group):
                _, _, qg, intra_bd, kdecs = setup[k]
                iuw = _dot(intra_bd, uw[k])
                oprime_scr[ci] = iuw[:, :DN_HEAD_DIM]
                qprime = (qg - iuw[:, DN_HEAD_DIM:]).astype(BF16)
                for hd in range(DN_HEADS):
                    hs = slice(hd * CHUNK, (hd + 1) * CHUNK)
                    kuw = _dot_tn(kdecs[hd], uw[k][hs])
                    hmat_scr[ci * DN_HEADS + hd] = kuw[:, :DN_HEAD_DIM]
                    gq_scr[ci * DN_HEADS + hd] = jnp.concatenate(
                        [(-kuw[:, DN_HEAD_DIM:]).astype(BF16), qprime[hs]], axis=0)
                yield

    def side_projection_steps():
        for pr in range(SSM_PAIRS):
            ssm_state_in(pr)
            yield

    _weave(delta_setup_steps(), side_projection_steps())

    lre = lam_ref[0:1, :]
    lim = lam_ref[1:2, :]
    sre = carry_scr[0:1, :]
    sim = carry_scr[1:2, :]
    for cb in range(nblk):
        sre_scr[cb:cb + 1, :] = sre
        sim_scr[cb:cb + 1, :] = sim
        bre = bre_scr[cb:cb + 1, :]
        bim = bim_scr[cb:cb + 1, :]
        sre, sim = lre * sre - lim * sim + bre, lre * sim + lim * sre + bim
    carry_scr[0:1, :] = sre
    carry_scr[1:2, :] = sim

    def delta_recurrence_steps():
        for ci in range(nchunk):
            rows = slice(ci * CHUNK, (ci + 1) * CHUNK)
            oprime = oprime_scr[ci]
            for hd in range(DN_HEADS):
                hs = slice(hd * CHUNK, (hd + 1) * CHUNK)
                idx = ci * DN_HEADS + hd
                s_old = state_scr[hd]
                gs = _dot(gq_scr[idx], s_old.astype(BF16))
                state_scr[hd] = s_old * elast_scr[ci, hd:hd + 1, :] + gs[:DN_HEAD_DIM] + hmat_scr[idx]
                odn_scr[rows, hd * DN_HEAD_DIM:(hd + 1) * DN_HEAD_DIM] = gs[DN_HEAD_DIM:] + oprime[hs]
            yield

    def ssm_output_steps():
        for pr in range(SSM_PAIRS):
            ssm_state_out(pr)
            if pr % 2 == 1:
                yield
        for vcol in range(SSM_WIDTH // LANES):
            outs = _piece_transpose8(
                [yp_scr[(vcol * 8 + g8) // 2, :, (g8 % 2) * SSM_ROW:(g8 % 2 + 1) * SSM_ROW] for g8 in range(8)])
            for i8 in range(SSM_L):
                y_scr[vcol, pl.ds(i8, nblk, stride=SSM_L), :] = outs[i8]
        y_val = jnp.concatenate([y_scr[vcol] for vcol in range(SSM_WIDTH // LANES)], axis=1)
        ge = _gelu_tanh(y_val).astype(BF16)
        gl = _dot(ge, gluw_ref[...]) + glub_ref[...]
        ossm_scr[...] = (gl[:, :SSM_WIDTH] * _sigmoid(gl[:, SSM_WIDTH:]) * _silu(zs_scr[...])).astype(BF16)
        yield

    _weave(delta_recurrence_steps(), ssm_output_steps())

    o_parts = []
    for hd in range(DN_HEADS):
        sl = slice(hd * DN_HEAD_DIM, (hd + 1) * DN_HEAD_DIM)
        oh = odn_scr[:, sl]
        on = oh * lax.rsqrt(jnp.mean(oh * oh, axis=-1, keepdims=True) + EPS) * normg_ref[...]
        o_parts.append((on * _silu(zdn_scr[:, sl])).astype(BF16))
    mix_in = jnp.concatenate(o_parts + [ossm_scr[...]], axis=1)
    mix = _dot(mix_in, wout_ref[...])
    mn = mix * lax.rsqrt(jnp.mean(mix * mix, axis=-1, keepdims=True) + EPS) * post_g_ref[...]
    o_ref[0] = x_ref[0] + mn


def _ssm_matrices(a_re, a_im, b_re, b_im, c_re, c_im, d, log_dt):
    hp = lax.Precision.HIGHEST
    dt = jnp.exp(log_dt.astype(F32))[:, None]
    a_re = a_re.astype(F32)
    a_im = a_im.astype(F32)
    mag = jnp.exp(a_re * dt)
    ang = a_im * dt
    lb_re, lb_im = mag * jnp.cos(ang), mag * jnp.sin(ang)
    den = a_re * a_re + a_im * a_im
    num_re, num_im = lb_re - 1.0, lb_im
    coef_re = (num_re * a_re + num_im * a_im) / den
    coef_im = (num_im * a_re - num_re * a_im) / den
    b_re = jnp.swapaxes(b_re.astype(F32), 1, 2)
    b_im = jnp.swapaxes(b_im.astype(F32), 1, 2)
    bb_re = coef_re[:, None, :] * b_re - coef_im[:, None, :] * b_im
    bb_im = coef_re[:, None, :] * b_im + coef_im[:, None, :] * b_re
    taus = jnp.arange(SSM_L + 1, dtype=F32)[None, :, None]
    pmag = jnp.exp((a_re * dt)[:, None, :] * taus)
    pang = (a_im * dt)[:, None, :] * taus
    lp_re, lp_im = pmag * jnp.cos(pang), pmag * jnp.sin(pang)
    c_re = c_re.astype(F32)
    c_im = c_im.astype(F32)
    cl_re = c_re[:, None] * lp_re[:, :, None, :] - c_im[:, None] * lp_im[:, :, None, :]
    cl_im = c_re[:, None] * lp_im[:, :, None, :] + c_im[:, None] * lp_re[:, :, None, :]
    kern = (jnp.einsum("gthp,gkp->tghk", cl_re[:, :SSM_L], bb_re, precision=hp)
            - jnp.einsum("gthp,gkp->tghk", cl_im[:, :SSM_L], bb_im, precision=hp))
    dmat = d.astype(F32).reshape(SSM_GROUPS, SSM_GROUP)[:, :, None] * jnp.eye(SSM_GROUP, dtype=F32)[None]
    kern = kern + jnp.asarray(_LAG0)[:, None, None, None] * dmat[None]
    mg = jnp.einsum("tji,tghk->gjkih", jnp.asarray(_LAG_SELECT), kern,
                    precision=hp).reshape(SSM_GROUPS, SSM_ROW, SSM_ROW)
    lr = lp_re[:, SSM_L - 1::-1][:, :SSM_L]
    li = lp_im[:, SSM_L - 1::-1][:, :SSM_L]
    bp_re = (lr[:, :, None, :] * bb_re[:, None] - li[:, :, None, :] * bb_im[:, None]
             ).reshape(SSM_GROUPS, SSM_ROW, SSM_STATE)
    bp_im = (lr[:, :, None, :] * bb_im[:, None] + li[:, :, None, :] * bb_re[:, None]
             ).reshape(SSM_GROUPS, SSM_ROW, SSM_STATE)
    co_re = jnp.transpose(cl_re[:, 1:], (0, 3, 1, 2)).reshape(SSM_GROUPS, SSM_STATE, SSM_ROW)
    co_im = -jnp.transpose(cl_im[:, 1:], (0, 3, 1, 2)).reshape(SSM_GROUPS, SSM_STATE, SSM_ROW)

    def pair(x):
        return x.reshape((SSM_PAIRS, 2) + x.shape[1:])

    def blocks(rows):
        return jnp.concatenate([jnp.concatenate(r, axis=2) for r in rows], axis=1)

    mp, bre, bim, cre, cim = pair(mg), pair(bp_re), pair(bp_im), pair(co_re), pair(co_im)
    zm, zb, zc = jnp.zeros_like(mp[:, 0]), jnp.zeros_like(bre[:, 0]), jnp.zeros_like(cre[:, 0])
    mssm = blocks([[mp[:, 0], zm], [zm, mp[:, 1]]])
    bpow = blocks([[bre[:, 0], zb, bim[:, 0], zb], [zb, bre[:, 1], zb, bim[:, 1]]])
    cpow = blocks([[cre[:, 0], zc], [zc, cre[:, 1]], [cim[:, 0], zc], [zc, cim[:, 1]]])
    lam = jnp.stack([lp_re[:, SSM_L].reshape(-1), lp_im[:, SSM_L].reshape(-1)], axis=0)
    return mssm.astype(BF16), bpow.astype(BF16), cpow.astype(BF16), lam


def _lag_tables():
    tau, j, i = np.meshgrid(np.arange(SSM_L), np.arange(SSM_L), np.arange(SSM_L), indexing="ij")
    return (i - j == tau).astype(np.float32), (np.arange(SSM_L) == 0).astype(np.float32)


_LAG_SELECT, _LAG0 = _lag_tables()


def _constants():
    ri = np.arange(CHUNK)[:, None]
    cj = np.arange(HROWS)[None, :] % CHUNK
    return tuple(jnp.asarray(m, dtype=F32) for m in (ri >= cj, ri > cj, ri == cj))


def _split_w_in(w_in):
    c_qkv = 3 * DN_WIDTH
    c_z = c_qkv + DN_WIDTH
    c_alpha = c_z + 2 * DN_HEADS
    c_u = c_alpha + SSM_WIDTH
    wb = w_in.astype(BF16)
    pad = [(0, 0)] * (w_in.ndim - 1) + [(0, LANES - 2 * DN_HEADS)]
    return (wb[..., :c_qkv], wb[..., c_qkv:c_z], wb[..., c_alpha:c_u], wb[..., c_u:],
            jnp.pad(wb[..., c_z:c_alpha], pad))


def _layer(x, ts, pre_g, post_g, w_parts, conv_w, a_log, dt_bias, norm_g, ssm, glu_w, glu_b, w_out):
    bsz, s, d = x.shape
    c_qkv = 3 * DN_WIDTH
    wqkv, wz, wu, wzs, wsm = w_parts
    dnvec = jnp.pad(jnp.stack([-jnp.exp(a_log.astype(F32)), dt_bias.astype(F32)]),
                    ((0, 0), (DN_HEADS, LANES - 2 * DN_HEADS)))
    mssm, bpow, cpow, lam = ssm
    consts = _constants()
    nblk = ts // SSM_L
    nstate = SSM_GROUPS * SSM_STATE

    def full(a):
        nd = a.ndim
        return pl.BlockSpec(a.shape, lambda b, t, _nd=nd: (0,) * _nd)

    operands = [
        pre_g.reshape(1, d).astype(F32), post_g.reshape(1, d).astype(F32),
        wqkv, wz, wu, wzs, wsm, conv_w.astype(F32), dnvec, norm_g.reshape(1, DN_HEAD_DIM).astype(F32),
        *consts, mssm, bpow, cpow, lam,
        glu_w, glu_b.reshape(1, -1).astype(F32), w_out]
    scratch = [
        pltpu.VMEM((ts, d), BF16),
        pltpu.VMEM((ts + CONV_HDR, c_qkv), F32),
        pltpu.VMEM((ts, DN_WIDTH), F32),
        pltpu.VMEM((ts, DN_WIDTH), F32),
        pltpu.VMEM((ts, DN_WIDTH), F32),
        pltpu.VMEM((ts, LANES), F32),
        pltpu.VMEM((ts, DN_WIDTH), F32),
        pltpu.VMEM((DN_HEADS, DN_HEAD_DIM, DN_HEAD_DIM), F32),
        pltpu.VMEM((ts // CHUNK, HROWS, DN_HEAD_DIM), F32),
        pltpu.VMEM((ts // CHUNK * DN_HEADS, DN_HEAD_DIM + CHUNK, DN_HEAD_DIM), BF16),
        pltpu.VMEM((ts // CHUNK * DN_HEADS, DN_HEAD_DIM, DN_HEAD_DIM), F32),
        pltpu.VMEM((ts // CHUNK, 8, DN_HEAD_DIM), F32),
        pltpu.VMEM((SSM_WIDTH // LANES, ts, LANES), F32),
        pltpu.VMEM((SSM_WIDTH // LANES, ts, LANES), F32),
        pltpu.VMEM((SSM_PAIRS, nblk, PAIR_ROW), BF16),
        pltpu.VMEM((SSM_PAIRS, nblk, PAIR_ROW), F32),
        pltpu.VMEM((ts, DN_WIDTH), F32),
        pltpu.VMEM((ts, SSM_WIDTH), F32),
        pltpu.VMEM((ts, SSM_WIDTH), BF16),
        pltpu.VMEM((nblk, nstate), F32),
        pltpu.VMEM((nblk, nstate), F32),
        pltpu.VMEM((nblk, nstate), F32),
        pltpu.VMEM((nblk, nstate), F32),
        pltpu.VMEM((2, nstate), F32),
    ]
    return pl.pallas_call(
        functools.partial(_layer_kernel, ts=ts),
        out_shape=jax.ShapeDtypeStruct(x.shape, x.dtype),
        grid=(bsz, s // ts),
        in_specs=[pl.BlockSpec((1, ts, d), lambda b, t: (b, t, 0))] + [full(a) for a in operands],
        out_specs=pl.BlockSpec((1, ts, d), lambda b, t: (b, t, 0)),
        scratch_shapes=scratch,
        compiler_params=pltpu.CompilerParams(
            dimension_semantics=("arbitrary", "arbitrary"),
            vmem_limit_bytes=VMEM_LIMIT_BYTES),
        name="hybrid_layer",
    )(x, *operands)


def _pick_tile(s):
    for ts in (512, 256, 128, 64):
        if s % ts == 0:
            return ts
    raise ValueError(f"sequence length {s} must be a multiple of {CHUNK}")


def kernel(x, pre_norm_g, post_norm_g, w_in, conv_w, dn_a_log, dn_dt_bias, dn_norm_g, ssm_a_re, ssm_a_im, ssm_b_re, ssm_b_im, ssm_c_re, ssm_c_im, ssm_d, ssm_log_dt, glu_w, glu_b, w_out):
    depth = w_in.shape[0]
    ts = _pick_tile(x.shape[1])
    ssm = jax.vmap(_ssm_matrices)(ssm_a_re, ssm_a_im, ssm_b_re, ssm_b_im, ssm_c_re, ssm_c_im,
                                  ssm_d, ssm_log_dt)
    w_parts = _split_w_in(w_in)
    glu_wb = glu_w.astype(BF16)
    w_outb = w_out.astype(BF16)
    for l in range(depth):
        x = _layer(x, ts, pre_norm_g[l], post_norm_g[l], [w[l] for w in w_parts], conv_w[l], dn_a_log[l],
                   dn_dt_bias[l], dn_norm_g[l], [m[l] for m in ssm], glu_wb[l], glu_b[l], w_outb[l])
    return x
```

```python
import functools
import math

import jax
import jax.numpy as jnp
import numpy as np
from jax import lax
from jax.experimental import pallas as pl
from jax.experimental.pallas import tpu as pltpu

F32 = jnp.float32
BF16 = jnp.bfloat16

DN_HEADS = 4
DN_HEAD_DIM = 128
DN_WIDTH = DN_HEADS * DN_HEAD_DIM
CONV_WIDTH = 4
CHUNK = 64
HROWS = DN_HEADS * CHUNK
SSM_GROUP = 16
SSM_GROUPS = 32
SSM_STATE = 64
SSM_WIDTH = SSM_GROUP * SSM_GROUPS
SSM_L = 8
SSM_PAIRS = SSM_GROUPS // 2
SSM_ROW = SSM_L * SSM_GROUP
PAIR_ROW = 2 * SSM_ROW
EPS = 1e-6
NEG_LOG2E = -1.4426950408889634
LANES = 128
NORM_ROWS = 32
GDN_GROUP = 8
CONV_HDR = 8
VMEM_LIMIT_BYTES = 56 * 1024 * 1024


def _dot(a, b):
    return jnp.dot(a, b, preferred_element_type=F32)


def _dot_nt(a, b):
    return lax.dot_general(a, b, (((1,), (1,)), ((), ())), preferred_element_type=F32)


def _dot_tn(a, b):
    return lax.dot_general(a, b, (((0,), (0,)), ((), ())), preferred_element_type=F32)


def _split2(a):
    hi = a.astype(BF16)
    mid = (a - hi.astype(F32)).astype(BF16)
    return hi, mid


def _sigmoid(x):
    return 1.0 / (1.0 + jnp.exp2(x * NEG_LOG2E))


def _silu(x):
    return x * _sigmoid(x)


def _softplus(x):
    return jnp.maximum(x, 0.0) + jnp.log(1.0 + jnp.exp(-jnp.abs(x)))


def _gelu_tanh(x):
    c = math.sqrt(2.0 / math.pi)
    return 0.5 * x * (1.0 + jnp.tanh(c * (x + 0.044715 * (x * x * x))))


def _piece_transpose8(vs):
    lane = lax.broadcasted_iota(jnp.int32, vs[0].shape, 1)
    piece = lane // SSM_GROUP
    for d in (4, 2, 1):
        keep = (piece & d) == 0
        new = list(vs)
        for a in range(8):
            if a & d == 0:
                b = a + d
                new[a] = jnp.where(keep, vs[a], pltpu.roll(vs[b], d * SSM_GROUP, 1))
                new[b] = jnp.where(keep, pltpu.roll(vs[a], LANES - d * SSM_GROUP, 1), vs[b])
        vs = new
    return vs


_DONE = object()


def _weave(main, side):
    side_done = False
    for _ in main:
        if not side_done:
            side_done = next(side, _DONE) is _DONE
    for _ in side:
        pass


def _layer_kernel(x_ref, pre_g_ref, post_g_ref, wqkv_ref, wz_ref, wu_ref, wzs_ref, wsm_ref,
                  conv_ref, dnvec_ref, normg_ref, causalcat_ref, strictcat_ref,
                  eyecat_ref, mssm_ref, bpow_ref, cpow_ref, lam_ref, gluw_ref, glub_ref, wout_ref,
                  o_ref,
                  h_scr, qkv_scr, q_scr, k_scr, v_scr, gates_scr, mixin_scr, state_scr,
                  oprime_scr, gq_scr, hmat_scr, elast_scr,
                  u_scr, y_scr, xp_scr, yp_scr, zdn_scr, zs_scr,
                  bre_scr, bim_scr, sre_scr, sim_scr, carry_scr, *, ts):
    t = pl.program_id(1)
    nchunk = ts // CHUNK
    nblk = ts // SSM_L

    @pl.when(t == 0)
    def _():
        qkv_scr[0:CONV_HDR, :] = jnp.zeros((CONV_HDR, 3 * DN_WIDTH), F32)
        state_scr[...] = jnp.zeros(state_scr.shape, F32)
        carry_scr[...] = jnp.zeros(carry_scr.shape, F32)

    for r in range(0, ts, NORM_ROWS):
        xb = x_ref[0, r:r + NORM_ROWS, :]
        ms = jnp.mean(xb * xb, axis=-1, keepdims=True)
        h_scr[r:r + NORM_ROWS, :] = (xb * lax.rsqrt(ms + EPS) * pre_g_ref[...]).astype(BF16)
    h = h_scr[...]

    small = _dot(h, wsm_ref[...])
    qkv_scr[CONV_HDR:CONV_HDR + ts, :] = _dot(h, wqkv_ref[...])
    u_val = _dot(h, wu_ref[...])
    for vcol in range(SSM_WIDTH // LANES):
        u_scr[vcol] = u_val[:, vcol * LANES:(vcol + 1) * LANES]
    zdn_scr[...] = _dot(h, wz_ref[...])

    for cb in range(3 * DN_WIDTH // LANES):
        cols = slice(cb * LANES, (cb + 1) * LANES)
        for r in range(0, ts, CHUNK):
            xx = qkv_scr[r:r + CONV_HDR + CHUNK, cols]
            acc = conv_ref[CONV_WIDTH - 1:CONV_WIDTH, cols] * xx
            for j in range(CONV_WIDTH - 1):
                acc = acc + pltpu.roll(conv_ref[j:j + 1, cols] * xx, CONV_WIDTH - 1 - j, 0)
            c = _silu(acc[CONV_HDR:])
            if cb < DN_HEADS:
                c = c * (lax.rsqrt(jnp.sum(c * c, axis=-1, keepdims=True) + EPS) * (DN_HEAD_DIM ** -0.5))
                q_scr[r:r + CHUNK, cols] = c
            elif cb < 2 * DN_HEADS:
                c = c * lax.rsqrt(jnp.sum(c * c, axis=-1, keepdims=True) + EPS)
                k_scr[r:r + CHUNK, cb * LANES - DN_WIDTH:(cb + 1) * LANES - DN_WIDTH] = c
            else:
                v_scr[r:r + CHUNK, cb * LANES - 2 * DN_WIDTH:(cb + 1) * LANES - 2 * DN_WIDTH] = c
    qkv_scr[0:CONV_HDR, :] = qkv_scr[ts:ts + CONV_HDR, :]

    lane = lax.broadcasted_iota(jnp.int32, small.shape, 1)
    is_g = (lane >= DN_HEADS) & (lane < 2 * DN_HEADS)
    neg_a = dnvec_ref[0:1, :]
    dt_b = dnvec_ref[1:2, :]
    gates_scr[...] = jnp.where(is_g, neg_a * _softplus(small + dt_b), _sigmoid(small))
    zs_scr[...] = _dot(h, wzs_ref[...])

    for vcol in range(SSM_WIDTH // LANES):
        outs = _piece_transpose8(
            [u_scr[vcol, pl.ds(j8, nblk, stride=SSM_L), :] for j8 in range(SSM_L)])
        for g8 in range(8):
            g = vcol * 8 + g8
            xp_scr[g // 2, :, (g % 2) * SSM_ROW:(g % 2 + 1) * SSM_ROW] = outs[g8].astype(BF16)

    def ssm_state_in(pr):
        bpair = _dot(xp_scr[pr], bpow_ref[pr])
        bre_scr[:, pr * LANES:(pr + 1) * LANES] = bpair[:, :LANES]
        bim_scr[:, pr * LANES:(pr + 1) * LANES] = bpair[:, LANES:]

    def ssm_state_out(pr):
        sp = jnp.concatenate([sre_scr[:, pr * LANES:(pr + 1) * LANES],
                              sim_scr[:, pr * LANES:(pr + 1) * LANES]], axis=1).astype(BF16)
        yp_scr[pr] = _dot(sp, cpow_ref[pr]) + _dot(xp_scr[pr], mssm_ref[pr])

    lane_c = lax.broadcasted_iota(jnp.int32, (CHUNK, LANES), 1)
    low_half = lane_c < CHUNK

    def blockdiag(xc):
        xb = xc.astype(BF16)
        zero = jnp.zeros((CHUNK, LANES), BF16)
        x01, x23 = xb[:, :LANES], xb[:, LANES:]
        return jnp.concatenate([
            jnp.concatenate([jnp.where(low_half, x01, zero), zero], axis=1),
            jnp.concatenate([jnp.where(low_half, zero, x01), zero], axis=1),
            jnp.concatenate([zero, jnp.where(low_half, x23, zero)], axis=1),
            jnp.concatenate([zero, jnp.where(low_half, zero, x23)], axis=1)], axis=0)

    def headcat(m):
        return jnp.concatenate(
            [jnp.where(low_half, m[0:CHUNK, :LANES], m[CHUNK:2 * CHUNK, :LANES]),
             jnp.where(low_half, m[2 * CHUNK:3 * CHUNK, LANES:], m[3 * CHUNK:4 * CHUNK, LANES:])], axis=1)

    row_c = lax.broadcasted_iota(jnp.int32, (CHUNK, LANES), 0)

    def chunk_cumsum(x):
        for sh in (1, 2, 4, 8, 16, 32):
            x = x + jnp.where(row_c >= sh, pltpu.roll(x, sh, 0), 0.0)
        return x

    def lane_bcast(x, lane_idx):
        return jnp.broadcast_to(x[:, lane_idx:lane_idx + 1], (CHUNK, LANES))

    def chunk_setup(ci):
        rows = slice(ci * CHUNK, (ci + 1) * CHUNK)
        gch = gates_scr[rows, :]
        gsum = chunk_cumsum(gch)
        gcols = [lane_bcast(gsum, DN_HEADS + hd) for hd in range(DN_HEADS)]
        bcols = [lane_bcast(gch, hd) for hd in range(DN_HEADS)]

        def stack(ref):
            return jnp.concatenate(
                [ref[rows, hd * DN_HEAD_DIM:(hd + 1) * DN_HEAD_DIM] for hd in range(DN_HEADS)], axis=0)

        kst = stack(k_scr)
        qst = stack(q_scr)
        vst = stack(v_scr)
        gcol = jnp.concatenate(gcols, axis=0)
        bcol = jnp.concatenate(bcols, axis=0)
        egc = jnp.exp(gcol)
        kst_b = kst.astype(BF16)
        bcol_b = bcol.astype(BF16)
        kb = kst_b * bcol_b
        vb = vst.astype(BF16) * bcol_b
        grow = jnp.transpose(gcol)[0:1, :]
        gcol_cat = jnp.concatenate(
            [jnp.where(low_half, gcols[0], gcols[1]), jnp.where(low_half, gcols[2], gcols[3])],
            axis=1)
        causal = causalcat_ref[...]
        decay = jnp.exp(jnp.where(causal > 0.0, gcol_cat - grow, 0.0)) * causal
        kk = _dot_nt(jnp.concatenate([kb, qst.astype(BF16)], axis=0), kst_b)
        a_cat = headcat(kk[:HROWS]) * decay * strictcat_ref[...]
        intra_bd = blockdiag(headcat(kk[HROWS:]) * decay)
        rhs = jnp.concatenate([vb, kb * egc.astype(BF16)], axis=1)
        qg = qst * egc
        kdecs = []
        for hd in range(DN_HEADS):
            hs = slice(hd * CHUNK, (hd + 1) * CHUNK)
            g_h = gcol[hs]
            g_last = g_h[CHUNK - 1:CHUNK, :]
            kdecs.append(kst_b[hs] * jnp.exp(g_last - g_h).astype(BF16))
            elast_scr[ci, hd:hd + 1, :] = jnp.exp(g_last)
        return a_cat, rhs, qg, intra_bd, kdecs

    def delta_setup_steps():
        for g0 in range(0, nchunk, GDN_GROUP):
            group = range(g0, min(g0 + GDN_GROUP, nchunk))
            setup = []
            for ci in group:
                setup.append(chunk_setup(ci))
                yield
            t_cats = [eyecat_ref[...] - st[0] for st in setup]
            p_cats = []
            for st in setup:
                p_cats.append(_dot(st[0].astype(BF16), blockdiag(st[0])))
            yield
            for lvl in range(5):
                for k in range(len(setup)):
                    p_bd = blockdiag(p_cats[k])
                    if lvl < 4:
                        both = _dot(jnp.concatenate([t_cats[k], p_cats[k]], axis=0).astype(BF16), p_bd)
                        t_cats[k] = t_cats[k] + both[:CHUNK]
                        p_cats[k] = both[CHUNK:]
                    else:
                        t_cats[k] = t_cats[k] + _dot(t_cats[k].astype(BF16), p_bd)
                yield
            uw = []
            for k in range(len(setup)):
                sol = _dot(blockdiag(t_cats[k]), setup[k][1])
                uw.append(sol.astype(BF16))
            yield
            for k, ci in enumerate(group):
                _, _, qg, intra_bd, kdecs = setup[k]
                iuw = _dot(intra_bd, uw[k])
                oprime_scr[ci] = iuw[:, :DN_HEAD_DIM]
                qprime = (qg - iuw[:, DN_HEAD_DIM:]).astype(BF16)
                for hd in range(DN_HEADS):
                    hs = slice(hd * CHUNK, (hd + 1) * CHUNK)
                    kuw = _dot_tn(kdecs[hd], uw[k][hs])
                    hmat_scr[ci * DN_HEADS + hd] = kuw[:, :DN_HEAD_DIM]
                    gq_scr[ci * DN_HEADS + hd] = jnp.concatenate(
                        [(-kuw[:, DN_HEAD_DIM:]).astype(BF16), qprime[hs]], axis=0)
                yield

    def side_projection_steps():
        for pr in range(SSM_PAIRS):
            ssm_state_in(pr)
            yield

    _weave(delta_setup_steps(), side_projection_steps())

    lre = lam_ref[0:1, :]
    lim = lam_ref[1:2, :]
    sre = carry_scr[0:1, :]
    sim = carry_scr[1:2, :]
    for cb in range(nblk):
        sre_scr[cb:cb + 1, :] = sre
        sim_scr[cb:cb + 1, :] = sim
        bre = bre_scr[cb:cb + 1, :]
        bim = bim_scr[cb:cb + 1, :]
        sre, sim = lre * sre - lim * sim + bre, lre * sim + lim * sre + bim
    carry_scr[0:1, :] = sre
    carry_scr[1:2, :] = sim

    def delta_recurrence_steps():
        for ci in range(nchunk):
            rows = slice(ci * CHUNK, (ci + 1) * CHUNK)
            oprime = oprime_scr[ci]
            for hd in range(DN_HEADS):
                hs = slice(hd * CHUNK, (hd + 1) * CHUNK)
                idx = ci * DN_HEADS + hd
                s_old = state_scr[hd]
                gs = _dot(gq_scr[idx], s_old.astype(BF16))
                state_scr[hd] = s_old * elast_scr[ci, hd:hd + 1, :] + gs[:DN_HEAD_DIM] + hmat_scr[idx]
                sl = slice(hd * DN_HEAD_DIM, (hd + 1) * DN_HEAD_DIM)
                oh = gs[DN_HEAD_DIM:] + oprime[hs]
                on = oh * lax.rsqrt(jnp.mean(oh * oh, axis=-1, keepdims=True) + EPS) * normg_ref[...]
                mixin_scr[rows, sl] = (on * _silu(zdn_scr[rows, sl])).astype(BF16)
            yield

    def ssm_output_steps():
        for pr in range(SSM_PAIRS):
            ssm_state_out(pr)
            if pr % 2 == 1:
                yield
        for vcol in range(SSM_WIDTH // LANES):
            outs = _piece_transpose8(
                [yp_scr[(vcol * 8 + g8) // 2, :, (g8 % 2) * SSM_ROW:(g8 % 2 + 1) * SSM_ROW] for g8 in range(8)])
            for i8 in range(SSM_L):
                y_scr[vcol, pl.ds(i8, nblk, stride=SSM_L), :] = outs[i8]
        y_val = jnp.concatenate([y_scr[vcol] for vcol in range(SSM_WIDTH // LANES)], axis=1)
        ge = _gelu_tanh(y_val).astype(BF16)
        gl = _dot(ge, gluw_ref[...]) + glub_ref[...]
        mixin_scr[:, DN_WIDTH:] = (gl[:, :SSM_WIDTH] * _sigmoid(gl[:, SSM_WIDTH:])
                                   * _silu(zs_scr[...])).astype(BF16)
        yield

    _weave(delta_recurrence_steps(), ssm_output_steps())

    mix = _dot(mixin_scr[...], wout_ref[...])
    mn = mix * lax.rsqrt(jnp.mean(mix * mix, axis=-1, keepdims=True) + EPS) * post_g_ref[...]
    o_ref[0] = x_ref[0] + mn


def _ssm_matrices(a_re, a_im, b_re, b_im, c_re, c_im, d, log_dt):
    hp = lax.Precision.HIGHEST
    dt = jnp.exp(log_dt.astype(F32))[:, None]
    a_re = a_re.astype(F32)
    a_im = a_im.astype(F32)
    mag = jnp.exp(a_re * dt)
    ang = a_im * dt
    lb_re, lb_im = mag * jnp.cos(ang), mag * jnp.sin(ang)
    den = a_re * a_re + a_im * a_im
    num_re, num_im = lb_re - 1.0, lb_im
    coef_re = (num_re * a_re + num_im * a_im) / den
    coef_im = (num_im * a_re - num_re * a_im) / den
    b_re = jnp.swapaxes(b_re.astype(F32), 1, 2)
    b_im = jnp.swapaxes(b_im.astype(F32), 1, 2)
    bb_re = coef_re[:, None, :] * b_re - coef_im[:, None, :] * b_im
    bb_im = coef_re[:, None, :] * b_im + coef_im[:, None, :] * b_re
    taus = jnp.arange(SSM_L + 1, dtype=F32)[None, :, None]
    pmag = jnp.exp((a_re * dt)[:, None, :] * taus)
    pang = (a_im * dt)[:, None, :] * taus
    lp_re, lp_im = pmag * jnp.cos(pang), pmag * jnp.sin(pang)
    c_re = c_re.astype(F32)
    c_im = c_im.astype(F32)
    cl_re = c_re[:, None] * lp_re[:, :, None, :] - c_im[:, None] * lp_im[:, :, None, :]
    cl_im = c_re[:, None] * lp_im[:, :, None, :] + c_im[:, None] * lp_re[:, :, None, :]
    kern = (jnp.einsum("gthp,gkp->tghk", cl_re[:, :SSM_L], bb_re, precision=hp)
            - jnp.einsum("gthp,gkp->tghk", cl_im[:, :SSM_L], bb_im, precision=hp))
    dmat = d.astype(F32).reshape(SSM_GROUPS, SSM_GROUP)[:, :, None] * jnp.eye(SSM_GROUP, dtype=F32)[None]
    kern = kern + jnp.asarray(_LAG0)[:, None, None, None] * dmat[None]
    mg = jnp.einsum("tji,tghk->gjkih", jnp.asarray(_LAG_SELECT), kern,
                    precision=hp).reshape(SSM_GROUPS, SSM_ROW, SSM_ROW)
    lr = lp_re[:, SSM_L - 1::-1][:, :SSM_L]
    li = lp_im[:, SSM_L - 1::-1][:, :SSM_L]
    bp_re = (lr[:, :, None, :] * bb_re[:, None] - li[:, :, None, :] * bb_im[:, None]
             ).reshape(SSM_GROUPS, SSM_ROW, SSM_STATE)
    bp_im = (lr[:, :, None, :] * bb_im[:, None] + li[:, :, None, :] * bb_re[:, None]
             ).reshape(SSM_GROUPS, SSM_ROW, SSM_STATE)
    co_re = jnp.transpose(cl_re[:, 1:], (0, 3, 1, 2)).reshape(SSM_GROUPS, SSM_STATE, SSM_ROW)
    co_im = -jnp.transpose(cl_im[:, 1:], (0, 3, 1, 2)).reshape(SSM_GROUPS, SSM_STATE, SSM_ROW)

    def pair(x):
        return x.reshape((SSM_PAIRS, 2) + x.shape[1:])

    def blocks(rows):
        return jnp.concatenate([jnp.concatenate(r, axis=2) for r in rows], axis=1)

    mp, bre, bim, cre, cim = pair(mg), pair(bp_re), pair(bp_im), pair(co_re), pair(co_im)
    zm, zb, zc = jnp.zeros_like(mp[:, 0]), jnp.zeros_like(bre[:, 0]), jnp.zeros_like(cre[:, 0])
    mssm = blocks([[mp[:, 0], zm], [zm, mp[:, 1]]])
    bpow = blocks([[bre[:, 0], zb, bim[:, 0], zb], [zb, bre[:, 1], zb, bim[:, 1]]])
    cpow = blocks([[cre[:, 0], zc], [zc, cre[:, 1]], [cim[:, 0], zc], [zc, cim[:, 1]]])
    lam = jnp.stack([lp_re[:, SSM_L].reshape(-1), lp_im[:, SSM_L].reshape(-1)], axis=0)
    return mssm.astype(BF16), bpow.astype(BF16), cpow.astype(BF16), lam


def _lag_tables():
    tau, j, i = np.meshgrid(np.arange(SSM_L), np.arange(SSM_L), np.arange(SSM_L), indexing="ij")
    return (i - j == tau).astype(np.float32), (np.arange(SSM_L) == 0).astype(np.float32)


_LAG_SELECT, _LAG0 = _lag_tables()


def _constants():
    ri = np.arange(CHUNK)[:, None]
    cj = np.arange(HROWS)[None, :] % CHUNK
    return tuple(jnp.asarray(m, dtype=F32) for m in (ri >= cj, ri > cj, ri == cj))


def _split_w_in(w_in):
    c_qkv = 3 * DN_WIDTH
    c_z = c_qkv + DN_WIDTH
    c_alpha = c_z + 2 * DN_HEADS
    c_u = c_alpha + SSM_WIDTH
    wb = w_in.astype(BF16)
    pad = [(0, 0)] * (w_in.ndim - 1) + [(0, LANES - 2 * DN_HEADS)]
    return (wb[..., :c_qkv], wb[..., c_qkv:c_z], wb[..., c_alpha:c_u], wb[..., c_u:],
            jnp.pad(wb[..., c_z:c_alpha], pad))


def _layer(x, ts, pre_g, post_g, w_parts, conv_w, a_log, dt_bias, norm_g, ssm, glu_w, glu_b, w_out):
    bsz, s, d = x.shape
    c_qkv = 3 * DN_WIDTH
    wqkv, wz, wu, wzs, wsm = w_parts
    dnvec = jnp.pad(jnp.stack([-jnp.exp(a_log.astype(F32)), dt_bias.astype(F32)]),
                    ((0, 0), (DN_HEADS, LANES - 2 * DN_HEADS)))
    mssm, bpow, cpow, lam = ssm
    consts = _constants()
    nblk = ts // SSM_L
    nstate = SSM_GROUPS * SSM_STATE

    def full(a):
        nd = a.ndim
        return pl.BlockSpec(a.shape, lambda b, t, _nd=nd: (0,) * _nd)

    operands = [
        pre_g.reshape(1, d).astype(F32), post_g.reshape(1, d).astype(F32),
        wqkv, wz, wu, wzs, wsm, conv_w.astype(F32), dnvec, norm_g.reshape(1, DN_HEAD_DIM).astype(F32),
        *consts, mssm, bpow, cpow, lam,
        glu_w, glu_b.reshape(1, -1).astype(F32), w_out]
    scratch = [
        pltpu.VMEM((ts, d), BF16),
        pltpu.VMEM((ts + CONV_HDR, c_qkv), F32),
        pltpu.VMEM((ts, DN_WIDTH), F32),
        pltpu.VMEM((ts, DN_WIDTH), F32),
        pltpu.VMEM((ts, DN_WIDTH), F32),
        pltpu.VMEM((ts, LANES), F32),
        pltpu.VMEM((ts, d), BF16),
        pltpu.VMEM((DN_HEADS, DN_HEAD_DIM, DN_HEAD_DIM), F32),
        pltpu.VMEM((ts // CHUNK, HROWS, DN_HEAD_DIM), F32),
        pltpu.VMEM((ts // CHUNK * DN_HEADS, DN_HEAD_DIM + CHUNK, DN_HEAD_DIM), BF16),
        pltpu.VMEM((ts // CHUNK * DN_HEADS, DN_HEAD_DIM, DN_HEAD_DIM), F32),
        pltpu.VMEM((ts // CHUNK, 8, DN_HEAD_DIM), F32),
        pltpu.VMEM((SSM_WIDTH // LANES, ts, LANES), F32),
        pltpu.VMEM((SSM_WIDTH // LANES, ts, LANES), F32),
        pltpu.VMEM((SSM_PAIRS, nblk, PAIR_ROW), BF16),
        pltpu.VMEM((SSM_PAIRS, nblk, PAIR_ROW), F32),
        pltpu.VMEM((ts, DN_WIDTH), F32),
        pltpu.VMEM((ts, SSM_WIDTH), F32),
        pltpu.VMEM((nblk, nstate), F32),
        pltpu.VMEM((nblk, nstate), F32),
        pltpu.VMEM((nblk, nstate), F32),
        pltpu.VMEM((nblk, nstate), F32),
        pltpu.VMEM((2, nstate), F32),
    ]
    return pl.pallas_call(
        functools.partial(_layer_kernel, ts=ts),
        out_shape=jax.ShapeDtypeStruct(x.shape, x.dtype),
        grid=(bsz, s // ts),
        in_specs=[pl.BlockSpec((1, ts, d), lambda b, t: (b, t, 0))] + [full(a) for a in operands],
        out_specs=pl.BlockSpec((1, ts, d), lambda b, t: (b, t, 0)),
        scratch_shapes=scratch,
        compiler_params=pltpu.CompilerParams(
            dimension_semantics=("arbitrary", "arbitrary"),
            vmem_limit_bytes=VMEM_LIMIT_BYTES),
        name="hybrid_layer",
    )(x, *operands)


def _pick_tile(s):
    for ts in (512, 256, 128, 64):
        if s % ts == 0:
            return ts
    raise ValueError(f"sequence length {s} must be a multiple of {CHUNK}")


def kernel(x, pre_norm_g, post_norm_g, w_in, conv_w, dn_a_log, dn_dt_bias, dn_norm_g, ssm_a_re, ssm_a_im, ssm_b_re, ssm_b_im, ssm_c_re, ssm_c_im, ssm_d, ssm_log_dt, glu_w, glu_b, w_out):
    depth = w_in.shape[0]
    ts = _pick_tile(x.shape[1])
    ssm = jax.vmap(_ssm_matrices)(ssm_a_re, ssm_a_im, ssm_b_re, ssm_b_im, ssm_c_re, ssm_c_im,
                                  ssm_d, ssm_log_dt)
    w_parts = _split_w_in(w_in)
    glu_wb = glu_w.astype(BF16)
    w_outb = w_out.astype(BF16)
    for l in range(depth):
        x = _layer(x, ts, pre_norm_g[l], post_norm_g[l], [w[l] for w in w_parts], conv_w[l], dn_a_log[l],
                   dn_dt_bias[l], dn_norm_g[l], [m[l] for m in ssm], glu_wb[l], glu_b[l], w_outb[l])
    return x
```
